```python
import math
import jax, jax.numpy as jnp
from jax import lax
import numpy as np

D_MODEL = 1024
BATCH = 4
SEQ = 4096
DEPTH = 1
DEC_BATCH = 8
DEC_SEQ = 2048
PAST_LEN = 128

HEAD_DIM = 64
N_HEADS_A = 8
N_HEADS_B = 8
N_KV_B = 2
WIDTH_A = N_HEADS_A * HEAD_DIM
WIDTH_B = N_HEADS_B * HEAD_DIM
KV_WIDTH_B = N_KV_B * HEAD_DIM
MIX_WIDTH = WIDTH_A + WIDTH_B
DILATED_PATTERNS = ((128, 1), (512, 4), (2048, 16))
GRID_W = 64
ROPE_THETA = 10000.0
Q_BLOCK = 128
LN_EPS = 1e-5
RMS_EPS = 1e-6
NEG_BIG = -1e30
DEEPNORM_ALPHA = (2.0 * DEPTH) ** 0.25
DEEPNORM_BETA = (8.0 * DEPTH) ** -0.25
IN_SIZES = (WIDTH_A, WIDTH_A, WIDTH_A, WIDTH_A, WIDTH_B, KV_WIDTH_B, KV_WIDTH_B, WIDTH_B)
IN_WIDTH = 4 * WIDTH_A + 2 * WIDTH_B + 2 * KV_WIDTH_B
IN_SPLITS = tuple(int(v) for v in np.cumsum(IN_SIZES)[:-1])

kernel_name = "hymba_dilated_gqa_axial_encoder"


def _rope_cos_sin(pos, dim):
    inv_freq = 1.0 / (ROPE_THETA ** (jnp.arange(0, dim, 2, dtype=jnp.float32) / dim))
    ang = pos.astype(jnp.float32)[:, None] * inv_freq[None, :]
    ang = jnp.concatenate([ang, ang], axis=-1)
    return jnp.cos(ang), jnp.sin(ang)


def _rope(x, cos, sin):
    xf = x.astype(jnp.float32)
    x1, x2 = jnp.split(xf, 2, axis=-1)
    rot = jnp.concatenate([-x2, x1], axis=-1)
    return (xf * cos[None, :, None, :] + rot * sin[None, :, None, :]).astype(x.dtype)


def _axial_rope(x, row, col):
    half = x.shape[-1] // 2
    cr, sr = _rope_cos_sin(row, half)
    cc, sc = _rope_cos_sin(col, half)
    return jnp.concatenate([_rope(x[..., :half], cr, sr), _rope(x[..., half:], cc, sc)], axis=-1)


def _rms_norm_heads(x, g):
    xf = x.astype(jnp.float32)
    xf = xf * lax.rsqrt(jnp.mean(xf * xf, axis=-1, keepdims=True) + RMS_EPS)
    return (xf * g.astype(jnp.float32)).astype(x.dtype)


def _dilated_attention(q, k, v):
    b, n, h, dh = q.shape
    scale = dh ** -0.5
    nb = n // Q_BLOCK
    qb = q.reshape(b, nb, Q_BLOCK, h, dh).transpose(1, 0, 2, 3, 4)
    starts = jnp.arange(nb, dtype=jnp.int32) * Q_BLOCK

    def block(args):
        q_blk, start = args
        qpos = start + jnp.arange(Q_BLOCK, dtype=jnp.int32)
        lses, outs = [], []
        for window, dil in DILATED_PATTERNS:
            half = window // (2 * dil)
            offs = dil * jnp.arange(-half, half + 1, dtype=jnp.int32)
            kpos = qpos[:, None] + offs[None, :]
            valid = (kpos >= 0) & (kpos < n)
            kidx = jnp.clip(kpos, 0, n - 1)
            kg = k[:, kidx]
            vg = v[:, kidx]
            s = jnp.einsum('bqhd,bqlhd->bhql', q_blk, kg,
                           preferred_element_type=jnp.float32) * scale
            s = jnp.where(valid[None, None], s, NEG_BIG)
            m = jnp.max(s, axis=-1, keepdims=True)
            p = jnp.exp(s - m)
            den = jnp.sum(p, axis=-1, keepdims=True)
            o = jnp.einsum('bhql,bqlhd->bqhd', p, vg.astype(jnp.float32))
            o = o / jnp.transpose(den, (0, 2, 1, 3))
            lse = jnp.transpose((m + jnp.log(den))[..., 0], (0, 2, 1))
            lses.append(lse)
            outs.append(o)
        w = jax.nn.softmax(jnp.stack(lses, axis=0), axis=0)
        out = jnp.sum(w[..., None] * jnp.stack(outs, axis=0), axis=0)
        return out.astype(q.dtype)

    out = lax.map(block, (qb, starts))
    return out.transpose(1, 0, 2, 3, 4).reshape(b, n, h, dh)


def _gqa_attention(q, k, v):
    b, n, hq, dh = q.shape
    hkv = k.shape[2]
    g = hq // hkv
    scale = dh ** -0.5
    nb = n // Q_BLOCK
    qb = q.reshape(b, nb, Q_BLOCK, hkv, g, dh).transpose(1, 0, 2, 3, 4, 5)

    def block(q_blk):
        s = jnp.einsum('bqkgd,bskd->bkgqs', q_blk, k,
                       preferred_element_type=jnp.float32) * scale
        p = jax.nn.softmax(s, axis=-1)
        o = jnp.einsum('bkgqs,bskd->bqkgd', p, v.astype(jnp.float32))
        return o.astype(q.dtype)

    out = lax.map(block, qb)
    return out.transpose(1, 0, 2, 3, 4, 5).reshape(b, n, hq, dh)


def _layer_norm(x, g, b):
    xf = x.astype(jnp.float32)
    mu = jnp.mean(xf, axis=-1, keepdims=True)
    var = jnp.mean(jnp.square(xf - mu), axis=-1, keepdims=True)
    y = (xf - mu) * lax.rsqrt(var + LN_EPS) * g.astype(jnp.float32) + b.astype(jnp.float32)
    return y.astype(x.dtype)


def _layer(x, w_in, w_out, q_norm, k_norm, ln_g, ln_b):
    b, n, _ = x.shape
    rows = n // GRID_W
    pos = jnp.arange(n, dtype=jnp.int32)
    row = jnp.repeat(jnp.arange(rows, dtype=jnp.int32), GRID_W)
    col = jnp.tile(jnp.arange(GRID_W, dtype=jnp.int32), rows)

    h = jnp.einsum('bnd,de->bne', x, w_in)
    qa, ka, va, ga, qb, kb, vb, gb = jnp.split(h, IN_SPLITS, axis=-1)

    cos, sin = _rope_cos_sin(pos, HEAD_DIM)
    qa = _rope(qa.reshape(b, n, N_HEADS_A, HEAD_DIM), cos, sin)
    ka = _rope(ka.reshape(b, n, N_HEADS_A, HEAD_DIM), cos, sin)
    va = va.reshape(b, n, N_HEADS_A, HEAD_DIM)
    out_a = _dilated_attention(qa, ka, va).reshape(b, n, WIDTH_A) * jax.nn.silu(ga)

    qb = _axial_rope(_rms_norm_heads(qb.reshape(b, n, N_HEADS_B, HEAD_DIM), q_norm), row, col)
    kb = _axial_rope(_rms_norm_heads(kb.reshape(b, n, N_KV_B, HEAD_DIM), k_norm), row, col)
    vb = vb.reshape(b, n, N_KV_B, HEAD_DIM)
    out_b = _gqa_attention(qb, kb, vb).reshape(b, n, WIDTH_B) * jax.nn.silu(gb)

    f = jnp.einsum('bne,ed->bnd', jnp.concatenate([out_a, out_b], axis=-1), w_out)
    return _layer_norm(DEEPNORM_ALPHA * x + f, ln_g, ln_b)


def setup_inputs(seed: int = 0) -> dict:
    key = jax.random.key(seed)
    ks = jax.random.split(key, 8)
    x_prompt = jax.random.normal(ks[0], (BATCH, SEQ, D_MODEL), jnp.float32)
    x_sample = jax.random.normal(ks[1], (DEC_BATCH, DEC_SEQ, D_MODEL), jnp.float32)
    col_scale = np.concatenate([
        np.ones(WIDTH_A), np.ones(WIDTH_A), np.full(WIDTH_A, DEEPNORM_BETA), np.ones(WIDTH_A),
        np.ones(WIDTH_B), np.ones(KV_WIDTH_B), np.full(KV_WIDTH_B, DEEPNORM_BETA), np.ones(WIDTH_B),
    ]).astype(np.float32)
    w_in = jax.random.normal(ks[2], (DEPTH, D_MODEL, IN_WIDTH), jnp.float32) * (D_MODEL ** -0.5)
    w_in = w_in * jnp.asarray(col_scale)[None, None, :]
    w_out = jax.random.normal(ks[3], (DEPTH, MIX_WIDTH, D_MODEL), jnp.float32) * (MIX_WIDTH ** -0.5) * DEEPNORM_BETA
    q_norm = 1.0 + 0.02 * jax.random.normal(ks[4], (DEPTH, HEAD_DIM), jnp.float32)
    k_norm = 1.0 + 0.02 * jax.random.normal(ks[5], (DEPTH, HEAD_DIM), jnp.float32)
    ln_g = 1.0 + 0.02 * jax.random.normal(ks[6], (DEPTH, D_MODEL), jnp.float32)
    ln_b = 0.02 * jax.random.normal(ks[7], (DEPTH, D_MODEL), jnp.float32)
    return {"x_prompt": x_prompt, "x_sample": x_sample, "w_in": w_in, "w_out": w_out,
            "q_norm": q_norm, "k_norm": k_norm, "ln_g": ln_g, "ln_b": ln_b}


def reference(x_prompt, x_sample, w_in, w_out, q_norm, k_norm, ln_g, ln_b):
    y_prompt = x_prompt
    y_sample = x_sample
    for i in range(DEPTH):
        y_prompt = _layer(y_prompt, w_in[i], w_out[i], q_norm[i], k_norm[i], ln_g[i], ln_b[i])
        y_sample = _layer(y_sample, w_in[i], w_out[i], q_norm[i], k_norm[i], ln_g[i], ln_b[i])
    return (y_prompt, y_sample)
```

```python
import functools
import math

import numpy as np
import jax
import jax.numpy as jnp
from jax import lax
from jax.experimental import pallas as pl
from jax.experimental.pallas import tpu as pltpu

D_MODEL = 1024
HEAD_DIM = 64
N_HEADS_A = 8
N_HEADS_B = 8
N_KV_B = 2
GROUP_B = N_HEADS_B // N_KV_B
WIDTH_A = N_HEADS_A * HEAD_DIM
WIDTH_B = N_HEADS_B * HEAD_DIM
KV_WIDTH_B = N_KV_B * HEAD_DIM
DILATED_PATTERNS = ((128, 1), (512, 4), (2048, 16))
GRID_W = 64
ROPE_THETA = 10000.0
LN_EPS = 1e-5
RMS_EPS = 1e-6
NEG_BIG = -1e30
DEPTH = 1
DEEPNORM_ALPHA = (2.0 * DEPTH) ** 0.25

OFF_QA = 0
OFF_KA = OFF_QA + WIDTH_A
OFF_VA = OFF_KA + WIDTH_A
OFF_GA = OFF_VA + WIDTH_A
OFF_QB = OFF_GA + WIDTH_A
OFF_KB = OFF_QB + WIDTH_B
OFF_VB = OFF_KB + KV_WIDTH_B
OFF_GB = OFF_VB + KV_WIDTH_B
IN_WIDTH = OFF_GB + WIDTH_B

LANES = 128
LOG2E = 1.4426950408889634
LN2 = 0.6931471805599453
SCORE_SCALE = (HEAD_DIM ** -0.5) * LOG2E

TOKEN_TILE = 512
GQA_TQ = 256
GQA_TK = 256
DIL_TQ = 128
DIL_HALF = 64
VMEM_LIMIT = 48 * 1024 * 1024

F32 = jnp.float32
BF16 = jnp.bfloat16


def _dot(a, b):
    return jnp.dot(a, b, preferred_element_type=F32)


def _dot_nt(a, b):
    return lax.dot_general(a, b, (((1,), (1,)), ((), ())), preferred_element_type=F32)


def _split_dot(x, w_bf16):
    hi = x.astype(BF16)
    lo = (x - hi.astype(F32)).astype(BF16)
    return _dot(hi, w_bf16) + _dot(lo, w_bf16)


def _rotate_half(x, first, half):
    return jnp.where(first, pltpu.roll(x, LANES - half, 1), pltpu.roll(x, half, 1))


def _inproj_kernel(x_ref, w_ref, ca_ref, sa_ref, cb_ref, sb_ref, qn_ref, kn_ref, bd_ref,
                   qa_ref, ka_ref, va_ref, ga_ref, qb_ref, gb_ref, kb_ref, vbt_ref):
    tm = x_ref.shape[0]
    xb = x_ref[...].astype(BF16)

    def proj(off, width):
        return _dot(xb, w_ref[:, off:off + width])

    lane = lax.broadcasted_iota(jnp.int32, (tm, LANES), 1)
    first_a = (lane % HEAD_DIM) < (HEAD_DIM // 2)
    first_b = (lane % (HEAD_DIM // 2)) < (HEAD_DIM // 4)
    ca, sa = ca_ref[...], sa_ref[...]
    cb, sb = cb_ref[...], sb_ref[...]
    bd = bd_ref[...]

    def rope_a(x):
        return x * ca + _rotate_half(x, first_a, HEAD_DIM // 2) * sa

    def norm_rope_b(x, gain):
        ms = _split_dot(x * x, bd) * (1.0 / HEAD_DIM)
        xn = x * lax.rsqrt(ms + RMS_EPS) * gain
        return xn * cb + _rotate_half(xn, first_b, HEAD_DIM // 4) * sb

    def silu(g):
        return g * (1.0 / (1.0 + jnp.exp(-g)))

    h = proj(OFF_QA, WIDTH_A)
    for c in range(WIDTH_A // LANES):
        qa_ref[:, c * LANES:(c + 1) * LANES] = rope_a(h[:, c * LANES:(c + 1) * LANES]).astype(BF16)
    h = proj(OFF_KA, WIDTH_A)
    for c in range(WIDTH_A // LANES):
        ka_ref[:, c * LANES:(c + 1) * LANES] = rope_a(h[:, c * LANES:(c + 1) * LANES]).astype(BF16)
    va_ref[...] = proj(OFF_VA, WIDTH_A).astype(BF16)
    ga_ref[...] = silu(proj(OFF_GA, WIDTH_A)).astype(BF16)

    h = proj(OFF_QB, WIDTH_B)
    qn = qn_ref[...]
    for c in range(WIDTH_B // LANES):
        qb_ref[:, c * LANES:(c + 1) * LANES] = norm_rope_b(h[:, c * LANES:(c + 1) * LANES], qn).astype(BF16)
    gb_ref[...] = silu(proj(OFF_GB, WIDTH_B)).astype(BF16)

    h = proj(OFF_KB, 2 * KV_WIDTH_B)
    kb = norm_rope_b(h[:, :KV_WIDTH_B], kn_ref[...]).astype(BF16)
    vbt = h[:, KV_WIDTH_B:].T
    for j in range(N_KV_B):
        kb_ref[j] = kb[:, j * HEAD_DIM:(j + 1) * HEAD_DIM]
        vbt_ref[j] = vbt[j * HEAD_DIM:(j + 1) * HEAD_DIM, :].astype(BF16)


def _rope_tables(n):
    def cos_sin(pos, dim):
        inv_freq = 1.0 / (ROPE_THETA ** (jnp.arange(0, dim, 2, dtype=F32) / dim))
        ang = pos.astype(F32)[:, None] * inv_freq[None, :]
        ang = jnp.concatenate([ang, ang], axis=-1)
        sign = jnp.concatenate([-jnp.ones((dim // 2,), F32), jnp.ones((dim // 2,), F32)])
        return jnp.cos(ang), jnp.sin(ang) * sign[None, :]

    pos = jnp.arange(n, dtype=jnp.int32)
    ca, sa = cos_sin(pos, HEAD_DIM)
    row = pos // GRID_W
    col = pos % GRID_W
    cr, sr = cos_sin(row, HEAD_DIM // 2)
    cc, sc = cos_sin(col, HEAD_DIM // 2)
    cb = jnp.concatenate([cr, cc], axis=-1)
    sb = jnp.concatenate([sr, sc], axis=-1)
    rep = LANES // HEAD_DIM
    return tuple(jnp.tile(t, (1, rep)) for t in (ca, sa, cb, sb))


def _inproj(x, w_bf16, tables, qn, kn, bd):
    b, n, _ = x.shape
    tm = TOKEN_TILE
    nt = n // tm
    t = b * n
    x2 = x.reshape(t, D_MODEL)
    row = lambda i: (i, 0)
    pos = lambda i: (i % nt, 0)
    const = lambda i: (0, 0)
    tab_spec = pl.BlockSpec((tm, LANES), pos)
    wide = lambda w: pl.BlockSpec((tm, w), row)
    out_shapes = (
        jax.ShapeDtypeStruct((t, WIDTH_A), BF16),
        jax.ShapeDtypeStruct((t, WIDTH_A), BF16),
        jax.ShapeDtypeStruct((t, WIDTH_A), BF16),
        jax.ShapeDtypeStruct((t, WIDTH_A), BF16),
        jax.ShapeDtypeStruct((t, WIDTH_B), BF16),
        jax.ShapeDtypeStruct((t, WIDTH_B), BF16),
        jax.ShapeDtypeStruct((b, N_KV_B, n, HEAD_DIM), BF16),
        jax.ShapeDtypeStruct((b, N_KV_B, HEAD_DIM, n), BF16),
    )
    out_specs = (
        wide(WIDTH_A), wide(WIDTH_A), wide(WIDTH_A), wide(WIDTH_A), wide(WIDTH_B), wide(WIDTH_B),
        pl.BlockSpec((None, N_KV_B, tm, HEAD_DIM), lambda i: (i // nt, 0, i % nt, 0)),
        pl.BlockSpec((None, N_KV_B, HEAD_DIM, tm), lambda i: (i // nt, 0, 0, i % nt)),
    )
    in_specs = [
        pl.BlockSpec((tm, D_MODEL), row),
        pl.BlockSpec((D_MODEL, IN_WIDTH), const),
        tab_spec, tab_spec, tab_spec, tab_spec,
        pl.BlockSpec((1, LANES), const),
        pl.BlockSpec((1, LANES), const),
        pl.BlockSpec((LANES, LANES), const),
    ]
    return pl.pallas_call(
        _inproj_kernel,
        grid=(t // tm,),
        in_specs=in_specs,
        out_specs=out_specs,
        out_shape=out_shapes,
        compiler_params=pltpu.CompilerParams(
            dimension_semantics=("arbitrary",), vmem_limit_bytes=VMEM_LIMIT),
        name="inproj",
    )(x2, w_bf16, *tables, qn, kn, bd)


def _gqa_kernel(q_ref, k_ref, vt_ref, g_ref, o_ref, qs_ref, m_ref, l_ref, acc_ref):
    tq = q_ref.shape[0]
    n = k_ref.shape[0]
    tk = GQA_TK
    for h in range(GROUP_B):
        qs_ref[h * tq:(h + 1) * tq, :] = q_ref[:, h * HEAD_DIM:(h + 1) * HEAD_DIM]
    m_ref[...] = jnp.full(m_ref.shape, NEG_BIG, F32)
    l_ref[...] = jnp.zeros(l_ref.shape, F32)
    acc_ref[...] = jnp.zeros(acc_ref.shape, F32)

    def body(c, carry):
        off = pl.multiple_of(c * tk, tk)
        k = k_ref[pl.ds(off, tk), :]
        s = _dot_nt(k, qs_ref[...]) * SCORE_SCALE
        m_prev = m_ref[...]
        m_new = jnp.maximum(m_prev, jnp.max(s, axis=0, keepdims=True))
        alpha = jnp.exp2(m_prev - m_new)
        p = jnp.exp2(s - m_new)
        l_ref[...] = alpha * l_ref[...] + jnp.sum(p, axis=0, keepdims=True)
        vt = vt_ref[:, pl.ds(off, tk)]
        acc_ref[...] = alpha * acc_ref[...] + _dot(vt, p.astype(BF16))
        m_ref[...] = m_new
        return carry

    lax.fori_loop(0, n // tk, body, 0)
    o_t = acc_ref[...] / l_ref[...]
    for h in range(GROUP_B):
        o_h = o_t[:, h * tq:(h + 1) * tq].T
        g_h = g_ref[:, h * HEAD_DIM:(h + 1) * HEAD_DIM].astype(F32)
        o_ref[:, h * HEAD_DIM:(h + 1) * HEAD_DIM] = (o_h * g_h).astype(BF16)


def _gqa(qb, kb, vbt, gb):
    b, n, _ = qb.shape
    tq = GQA_TQ
    gw = GROUP_B * HEAD_DIM
    tile = pl.BlockSpec((None, tq, gw), lambda bi, j, i: (bi, i, j))
    return pl.pallas_call(
        _gqa_kernel,
        grid=(b, N_KV_B, n // tq),
        in_specs=[
            tile,
            pl.BlockSpec((None, None, n, HEAD_DIM), lambda bi, j, i: (bi, j, 0, 0)),
            pl.BlockSpec((None, None, HEAD_DIM, n), lambda bi, j, i: (bi, j, 0, 0)),
            tile,
        ],
        out_specs=tile,
        out_shape=jax.ShapeDtypeStruct((b, n, WIDTH_B), BF16),
        scratch_shapes=[
            pltpu.VMEM((GROUP_B * tq, HEAD_DIM), BF16),
            pltpu.VMEM((1, GROUP_B * tq), F32),
            pltpu.VMEM((1, GROUP_B * tq), F32),
            pltpu.VMEM((HEAD_DIM, GROUP_B * tq), F32),
        ],
        compiler_params=pltpu.CompilerParams(
            dimension_semantics=("arbitrary", "arbitrary", "arbitrary"),
            vmem_limit_bytes=VMEM_LIMIT),
        name="gqa",
    )(qb, kb, vbt, gb)


def _dilated_kernel(q_ref, k_ref, v_ref, o_ref, lse_ref):
    length = q_ref.shape[0]
    tq = DIL_TQ
    win = min(tq + 2 * DIL_HALF, length)
    lane = lax.broadcasted_iota(jnp.int32, (tq, LANES), 1)
    low_head = lane < HEAD_DIM
    ones = jnp.ones((win, LANES), BF16)
    rel = (lax.broadcasted_iota(jnp.int32, (tq, win), 1)
           - lax.broadcasted_iota(jnp.int32, (tq, win), 0))

    def body(i, carry):
        a0 = pl.multiple_of(i * tq, tq)
        ws = pl.multiple_of(jnp.clip(a0 - DIL_HALF, 0, length - win), DIL_HALF)
        valid = jnp.abs(rel + (ws - a0)) <= DIL_HALF
        lse_tile = jnp.zeros((tq, LANES), F32)
        for pair in range(N_HEADS_A // 2):
            cols = slice(pair * LANES, (pair + 1) * LANES)
            q2 = q_ref[pl.ds(a0, tq), cols]
            k2 = k_ref[pl.ds(ws, win), cols]
            v2 = v_ref[pl.ds(ws, win), cols]
            outs = []
            for hh in range(2):
                mine = low_head if hh == 0 else jnp.logical_not(low_head)
                qh = jnp.where(mine, q2, jnp.zeros_like(q2))
                s = _dot_nt(qh, k2) * SCORE_SCALE
                s = jnp.where(valid, s, NEG_BIG)
                m = jnp.max(s, axis=-1, keepdims=True)
                p = jnp.exp2(s - m).astype(BF16)
                num = _dot(p, v2)
                den = _dot(p, ones)
                outs.append(num / den)
                lse = (m + jnp.log2(den)) * LN2
                lse_tile = jnp.where(lane == 2 * pair + hh, lse, lse_tile)
            o_ref[pl.ds(a0, tq), cols] = jnp.where(low_head, outs[0], outs[1]).astype(BF16)
        lse_ref[pl.ds(a0, tq), :] = lse_tile
        return carry

    lax.fori_loop(0, length // tq, body, 0)


def _dilated(qa, ka, va, dil):
    b, n, _ = qa.shape
    length = n // dil
    view = lambda a: a.reshape(b, length, dil * WIDTH_A)
    spec = pl.BlockSpec((None, length, WIDTH_A), lambda bi, r: (bi, 0, r))
    lse_spec = pl.BlockSpec((None, length, LANES), lambda bi, r: (bi, 0, r))
    o, lse = pl.pallas_call(
        _dilated_kernel,
        grid=(b, dil),
        in_specs=[spec, spec, spec],
        out_specs=(spec, lse_spec),
        out_shape=(jax.ShapeDtypeStruct((b, length, dil * WIDTH_A), BF16),
                   jax.ShapeDtypeStruct((b, length, dil * LANES), F32)),
        compiler_params=pltpu.CompilerParams(
            dimension_semantics=("arbitrary", "arbitrary"), vmem_limit_bytes=VMEM_LIMIT),
        name=f"dilated{dil}",
    )(view(qa), view(ka), view(va))
    return o.reshape(b * n, WIDTH_A), lse.reshape(b * n, LANES)


def _outproj_kernel(x_ref, o1_ref, o2_ref, o3_ref, l1_ref, l2_ref, l3_ref, ga_ref, mb_ref,
                    w_ref, ex_ref, lg_ref, lb_ref, y_ref):
    lses = (l1_ref[...], l2_ref[...], l3_ref[...])
    mx = jnp.maximum(jnp.maximum(lses[0], lses[1]), lses[2])
    es = [jnp.exp(l - mx) for l in lses]
    inv = 1.0 / (es[0] + es[1] + es[2])
    ex = ex_ref[...]
    out_a = None
    for e, o_ref in zip(es, (o1_ref, o2_ref, o3_ref)):
        wexp = _split_dot(e * inv, ex)
        term = wexp * o_ref[...].astype(F32)
        out_a = term if out_a is None else out_a + term
    mix_a = (out_a * ga_ref[...].astype(F32)).astype(BF16)
    f = _dot(mix_a, w_ref[:WIDTH_A, :]) + _dot(mb_ref[...], w_ref[WIDTH_A:, :])
    z = DEEPNORM_ALPHA * x_ref[...] + f
    mu = jnp.mean(z, axis=-1, keepdims=True)
    zc = z - mu
    var = jnp.mean(zc * zc, axis=-1, keepdims=True)
    y_ref[...] = zc * lax.rsqrt(var + LN_EPS) * lg_ref[...] + lb_ref[...]


def _outproj(x2, o_list, lse_list, ga, mix_b, w_out_bf16, expand, ln_g, ln_b):
    t = x2.shape[0]
    tm = TOKEN_TILE
    row = lambda i: (i, 0)
    const = lambda i: (0, 0)
    wide = lambda w: pl.BlockSpec((tm, w), row)
    return pl.pallas_call(
        _outproj_kernel,
        grid=(t // tm,),
        in_specs=[
            wide(D_MODEL),
            wide(WIDTH_A), wide(WIDTH_A), wide(WIDTH_A),
            wide(LANES), wide(LANES), wide(LANES),
            wide(WIDTH_A), wide(WIDTH_B),
            pl.BlockSpec((WIDTH_A + WIDTH_B, D_MODEL), const),
            pl.BlockSpec((LANES, WIDTH_A), const),
            pl.BlockSpec((1, D_MODEL), const),
            pl.BlockSpec((1, D_MODEL), const),
        ],
        out_specs=wide(D_MODEL),
        out_shape=jax.ShapeDtypeStruct((t, D_MODEL), F32),
        compiler_params=pltpu.CompilerParams(
            dimension_semantics=("arbitrary",), vmem_limit_bytes=VMEM_LIMIT),
        name="outproj",
    )(x2, *o_list, *lse_list, ga, mix_b, w_out_bf16, expand, ln_g, ln_b)


def _constants():
    head_of_lane = np.arange(LANES) // HEAD_DIM
    bd = (head_of_lane[:, None] == head_of_lane[None, :]).astype(np.float32)
    ex = (np.arange(LANES)[:, None] == (np.arange(WIDTH_A) // HEAD_DIM)[None, :]).astype(np.float32)
    return jnp.asarray(bd, BF16), jnp.asarray(ex, BF16)


def _layer(x, w_in_bf16, w_out_bf16, tables, qn, kn, ln_g, ln_b, bd, expand):
    b, n, _ = x.shape
    qa, ka, va, ga, qb, gb, kb, vbt = _inproj(x, w_in_bf16, tables, qn, kn, bd)
    mix_b = _gqa(qb.reshape(b, n, WIDTH_B), kb, vbt, gb.reshape(b, n, WIDTH_B))
    shape3 = lambda a: a.reshape(b, n, WIDTH_A)
    o_list, lse_list = [], []
    for _, dil in DILATED_PATTERNS:
        o, lse = _dilated(shape3(qa), shape3(ka), shape3(va), dil)
        o_list.append(o)
        lse_list.append(lse)
    y = _outproj(x.reshape(b * n, D_MODEL), o_list, lse_list, ga, mix_b.reshape(b * n, WIDTH_B),
                 w_out_bf16, expand, ln_g, ln_b)
    return y.reshape(b, n, D_MODEL)


def kernel(x_prompt, x_sample, w_in, w_out, q_norm, k_norm, ln_g, ln_b):
    bd, expand = _constants()
    tables = _rope_tables(max(x_prompt.shape[1], x_sample.shape[1]))
    rep = LANES // HEAD_DIM
    y_prompt, y_sample = x_prompt, x_sample
    for i in range(w_in.shape[0]):
        w_in_bf16 = w_in[i].astype(BF16)
        w_out_bf16 = w_out[i].astype(BF16)
        qn = jnp.tile(q_norm[i].reshape(1, HEAD_DIM), (1, rep))
        kn = jnp.tile(k_norm[i].reshape(1, HEAD_DIM), (1, rep))
        args = (w_in_bf16, w_out_bf16, tables, qn, kn,
                ln_g[i].reshape(1, D_MODEL), ln_b[i].reshape(1, D_MODEL), bd, expand)
        y_prompt = _layer(y_prompt, *args)
        y_sample = _layer(y_sample, *args)
    return (y_prompt, y_sample)
```

```python
import functools
import math

import numpy as np
import jax
import jax.numpy as jnp
from jax import lax
from jax.experimental import pallas as pl
from jax.experimental.pallas import tpu as pltpu

D_MODEL = 1024
HEAD_DIM = 64
N_HEADS_A = 8
N_HEADS_B = 8
N_KV_B = 2
GROUP_B = N_HEADS_B // N_KV_B
WIDTH_A = N_HEADS_A * HEAD_DIM
WIDTH_B = N_HEADS_B * HEAD_DIM
KV_WIDTH_B = N_KV_B * HEAD_DIM
DILATED_PATTERNS = ((128, 1), (512, 4), (2048, 16))
GRID_W = 64
ROPE_THETA = 10000.0
LN_EPS = 1e-5
RMS_EPS = 1e-6
NEG_BIG = -1e30
DEPTH = 1
DEEPNORM_ALPHA = (2.0 * DEPTH) ** 0.25

OFF_QA = 0
OFF_KA = OFF_QA + WIDTH_A
OFF_VA = OFF_KA + WIDTH_A
OFF_GA = OFF_VA + WIDTH_A
OFF_QB = OFF_GA + WIDTH_A
OFF_KB = OFF_QB + WIDTH_B
OFF_VB = OFF_KB + KV_WIDTH_B
OFF_GB = OFF_VB + KV_WIDTH_B
IN_WIDTH = OFF_GB + WIDTH_B

LANES = 128
LOG2E = 1.4426950408889634
LN2 = 0.6931471805599453
SCORE_SCALE = (HEAD_DIM ** -0.5) * LOG2E

TOKEN_TILE = 512
GQA_TQ = 256
GQA_TK = 512
BF16_ROWS = 16
DIL_TQ = 128
DIL_HALF = 64
VMEM_LIMIT = 48 * 1024 * 1024

F32 = jnp.float32
BF16 = jnp.bfloat16


def _dot(a, b):
    return jnp.dot(a, b, preferred_element_type=F32)


def _dot_nt(a, b):
    return lax.dot_general(a, b, (((1,), (1,)), ((), ())), preferred_element_type=F32)


def _split_dot(x, w_bf16):
    hi = x.astype(BF16)
    lo = (x - hi.astype(F32)).astype(BF16)
    return _dot(hi, w_bf16) + _dot(lo, w_bf16)


def _rotate_half(x, first, half):
    return jnp.where(first, pltpu.roll(x, LANES - half, 1), pltpu.roll(x, half, 1))


def _inproj_kernel(x_ref, w_ref, ca_ref, sa_ref, cb_ref, sb_ref, qn_ref, kn_ref, bd_ref,
                   qa_ref, ka_ref, va_ref, ga_ref, qb_ref, gb_ref, kb_ref, vbt_ref):
    tm = x_ref.shape[0]
    xb = x_ref[...].astype(BF16)

    def proj(off, width):
        return _dot(xb, w_ref[:, off:off + width])

    lane = lax.broadcasted_iota(jnp.int32, (tm, LANES), 1)
    first_a = (lane % HEAD_DIM) < (HEAD_DIM // 2)
    first_b = (lane % (HEAD_DIM // 2)) < (HEAD_DIM // 4)
    ca, sa = ca_ref[...], sa_ref[...]
    cb, sb = cb_ref[...], sb_ref[...]
    bd = bd_ref[...]

    def rope_a(x):
        return x * ca + _rotate_half(x, first_a, HEAD_DIM // 2) * sa

    def norm_rope_b(x, gain):
        ms = _split_dot(x * x, bd) * (1.0 / HEAD_DIM)
        xn = x * lax.rsqrt(ms + RMS_EPS) * gain
        return xn * cb + _rotate_half(xn, first_b, HEAD_DIM // 4) * sb

    def silu(g):
        return g * (1.0 / (1.0 + jnp.exp(-g)))

    h = proj(OFF_QA, WIDTH_A)
    for c in range(WIDTH_A // LANES):
        qa_ref[:, c * LANES:(c + 1) * LANES] = rope_a(h[:, c * LANES:(c + 1) * LANES]).astype(BF16)
    h = proj(OFF_KA, WIDTH_A)
    for c in range(WIDTH_A // LANES):
        ka_ref[:, c * LANES:(c + 1) * LANES] = rope_a(h[:, c * LANES:(c + 1) * LANES]).astype(BF16)
    va_ref[...] = proj(OFF_VA, WIDTH_A).astype(BF16)
    ga_ref[...] = silu(proj(OFF_GA, WIDTH_A)).astype(BF16)

    h = proj(OFF_QB, WIDTH_B)
    qn = qn_ref[...]
    for c in range(WIDTH_B // LANES):
        qb = norm_rope_b(h[:, c * LANES:(c + 1) * LANES], qn) * SCORE_SCALE
        qb_ref[:, c * LANES:(c + 1) * LANES] = qb.astype(BF16)
    gb_ref[...] = silu(proj(OFF_GB, WIDTH_B)).astype(BF16)

    h = proj(OFF_KB, 2 * KV_WIDTH_B)
    kb = norm_rope_b(h[:, :KV_WIDTH_B], kn_ref[...])
    kb_swapped = pltpu.roll(kb, HEAD_DIM, 1)
    low = lane < HEAD_DIM
    kb_ref[0] = jnp.where(low, kb, kb_swapped).astype(BF16)
    kb_ref[1] = jnp.where(low, kb_swapped, kb).astype(BF16)
    vbt = h[:, KV_WIDTH_B:].T
    for j in range(N_KV_B):
        vbt_ref[j] = vbt[j * HEAD_DIM:(j + 1) * HEAD_DIM, :].astype(BF16)


def _rope_tables(n):
    def cos_sin(pos, dim):
        inv_freq = 1.0 / (ROPE_THETA ** (jnp.arange(0, dim, 2, dtype=F32) / dim))
        ang = pos.astype(F32)[:, None] * inv_freq[None, :]
        ang = jnp.concatenate([ang, ang], axis=-1)
        sign = jnp.concatenate([-jnp.ones((dim // 2,), F32), jnp.ones((dim // 2,), F32)])
        return jnp.cos(ang), jnp.sin(ang) * sign[None, :]

    pos = jnp.arange(n, dtype=jnp.int32)
    ca, sa = cos_sin(pos, HEAD_DIM)
    row = pos // GRID_W
    col = pos % GRID_W
    cr, sr = cos_sin(row, HEAD_DIM // 2)
    cc, sc = cos_sin(col, HEAD_DIM // 2)
    cb = jnp.concatenate([cr, cc], axis=-1)
    sb = jnp.concatenate([sr, sc], axis=-1)
    rep = LANES // HEAD_DIM
    return tuple(jnp.tile(t, (1, rep)) for t in (ca, sa, cb, sb))


def _inproj(x, w_bf16, tables, qn, kn, bd):
    b, n, _ = x.shape
    tm = TOKEN_TILE
    nt = n // tm
    t = b * n
    x2 = x.reshape(t, D_MODEL)
    row = lambda i: (i, 0)
    pos = lambda i: (i % nt, 0)
    const = lambda i: (0, 0)
    tab_spec = pl.BlockSpec((tm, LANES), pos)
    wide = lambda w: pl.BlockSpec((tm, w), row)
    out_shapes = (
        jax.ShapeDtypeStruct((t, WIDTH_A), BF16),
        jax.ShapeDtypeStruct((t, WIDTH_A), BF16),
        jax.ShapeDtypeStruct((t, WIDTH_A), BF16),
        jax.ShapeDtypeStruct((t, WIDTH_A), BF16),
        jax.ShapeDtypeStruct((t, WIDTH_B), BF16),
        jax.ShapeDtypeStruct((t, WIDTH_B), BF16),
        jax.ShapeDtypeStruct((b, N_KV_B, n, LANES), BF16),
        jax.ShapeDtypeStruct((b, N_KV_B, HEAD_DIM, n), BF16),
    )
    out_specs = (
        wide(WIDTH_A), wide(WIDTH_A), wide(WIDTH_A), wide(WIDTH_A), wide(WIDTH_B), wide(WIDTH_B),
        pl.BlockSpec((None, N_KV_B, tm, LANES), lambda i: (i // nt, 0, i % nt, 0)),
        pl.BlockSpec((None, N_KV_B, HEAD_DIM, tm), lambda i: (i // nt, 0, 0, i % nt)),
    )
    in_specs = [
        pl.BlockSpec((tm, D_MODEL), row),
        pl.BlockSpec((D_MODEL, IN_WIDTH), const),
        tab_spec, tab_spec, tab_spec, tab_spec,
        pl.BlockSpec((1, LANES), const),
        pl.BlockSpec((1, LANES), const),
        pl.BlockSpec((LANES, LANES), const),
    ]
    return pl.pallas_call(
        _inproj_kernel,
        grid=(t // tm,),
        in_specs=in_specs,
        out_specs=out_specs,
        out_shape=out_shapes,
        compiler_params=pltpu.CompilerParams(
            dimension_semantics=("arbitrary",), vmem_limit_bytes=VMEM_LIMIT),
        name="inproj",
    )(x2, w_bf16, *tables, qn, kn, bd)


def _gqa_kernel(q_ref, k_ref, vt_ref, g_ref, o_ref, qs_ref, s0_ref, s1_ref, m_ref, acc_ref):
    tq = q_ref.shape[0]
    n = k_ref.shape[0]
    tk = GQA_TK
    nc = n // tk
    low = lax.broadcasted_iota(jnp.int32, (tq, LANES), 1) < HEAD_DIM
    for h in range(GROUP_B):
        pair = q_ref[:, (h // 2) * LANES:(h // 2 + 1) * LANES]
        mine = low if h % 2 == 0 else jnp.logical_not(low)
        qs_ref[h * tq:(h + 1) * tq, :] = jnp.where(mine, pair, jnp.zeros_like(pair))
    m_ref[...] = jnp.full(m_ref.shape, NEG_BIG, F32)
    acc_ref[...] = jnp.zeros(acc_ref.shape, F32)
    ones = jnp.ones((BF16_ROWS, tk), BF16)

    def scores(c):
        off = pl.multiple_of(c * tk, tk)
        return _dot_nt(k_ref[pl.ds(off, tk), :], qs_ref[...])

    def update(s, c):
        off = pl.multiple_of(c * tk, tk)
        m_prev = m_ref[...]
        m_new = jnp.maximum(m_prev, jnp.max(s, axis=0, keepdims=True))
        alpha = jnp.exp2(m_prev - m_new)
        p = jnp.exp2(s - m_new).astype(BF16)
        vt = jnp.concatenate([vt_ref[:, pl.ds(off, tk)], ones], axis=0)
        acc_ref[...] = alpha * acc_ref[...] + _dot(vt, p)
        m_ref[...] = m_new

    s0_ref[...] = scores(0)

    def body(cc, carry):
        c = 2 * cc
        s1_ref[...] = scores(c + 1)
        update(s0_ref[...], c)
        s0_ref[...] = scores(c + 2)
        update(s1_ref[...], c + 1)
        return carry

    lax.fori_loop(0, nc // 2 - 1, body, 0)
    s1_ref[...] = scores(nc - 1)
    update(s0_ref[...], nc - 2)
    update(s1_ref[...], nc - 1)

    acc = acc_ref[...]
    o_t = acc[:HEAD_DIM] * (1.0 / acc[HEAD_DIM:HEAD_DIM + 1])
    for pr in range(GROUP_B // 2):
        pair_t = jnp.concatenate([o_t[:, (2 * pr) * tq:(2 * pr + 1) * tq],
                                  o_t[:, (2 * pr + 1) * tq:(2 * pr + 2) * tq]], axis=0)
        gate = g_ref[:, pr * LANES:(pr + 1) * LANES].astype(F32)
        o_ref[:, pr * LANES:(pr + 1) * LANES] = (pair_t.T * gate).astype(BF16)


def _gqa(qb, kb, vbt, gb):
    b, n, _ = qb.shape
    tq = GQA_TQ
    gw = GROUP_B * HEAD_DIM
    assert (n // GQA_TK) % 2 == 0
    tile = pl.BlockSpec((None, tq, gw), lambda bi, j, i: (bi, i, j))
    return pl.pallas_call(
        _gqa_kernel,
        grid=(b, N_KV_B, n // tq),
        in_specs=[
            tile,
            pl.BlockSpec((None, None, n, LANES), lambda bi, j, i: (bi, j, 0, 0)),
            pl.BlockSpec((None, None, HEAD_DIM, n), lambda bi, j, i: (bi, j, 0, 0)),
            tile,
        ],
        out_specs=tile,
        out_shape=jax.ShapeDtypeStruct((b, n, WIDTH_B), BF16),
        scratch_shapes=[
            pltpu.VMEM((GROUP_B * tq, LANES), BF16),
            pltpu.VMEM((GQA_TK, GROUP_B * tq), F32),
            pltpu.VMEM((GQA_TK, GROUP_B * tq), F32),
            pltpu.VMEM((1, GROUP_B * tq), F32),
            pltpu.VMEM((HEAD_DIM + BF16_ROWS, GROUP_B * tq), F32),
        ],
        compiler_params=pltpu.CompilerParams(
            dimension_semantics=("arbitrary", "arbitrary", "arbitrary"),
            vmem_limit_bytes=VMEM_LIMIT),
        name="gqa",
    )(qb, kb, vbt, gb)


def _dilated_kernel(q_ref, k_ref, v_ref, o_ref, lse_ref):
    length = q_ref.shape[0]
    tq = DIL_TQ
    win = min(tq + 2 * DIL_HALF, length)
    lane = lax.broadcasted_iota(jnp.int32, (tq, LANES), 1)
    low_head = lane < HEAD_DIM
    ones = jnp.ones((win, LANES), BF16)
    rel = (lax.broadcasted_iota(jnp.int32, (tq, win), 1)
           - lax.broadcasted_iota(jnp.int32, (tq, win), 0))

    def body(i, carry):
        a0 = pl.multiple_of(i * tq, tq)
        ws = pl.multiple_of(jnp.clip(a0 - DIL_HALF, 0, length - win), DIL_HALF)
        valid = jnp.abs(rel + (ws - a0)) <= DIL_HALF
        lse_tile = jnp.zeros((tq, LANES), F32)
        for pair in range(N_HEADS_A // 2):
            cols = slice(pair * LANES, (pair + 1) * LANES)
            q2 = q_ref[pl.ds(a0, tq), cols]
            k2 = k_ref[pl.ds(ws, win), cols]
            v2 = v_ref[pl.ds(ws, win), cols]
            outs = []
            for hh in range(2):
                mine = low_head if hh == 0 else jnp.logical_not(low_head)
                qh = jnp.where(mine, q2, jnp.zeros_like(q2))
                s = _dot_nt(qh, k2) * SCORE_SCALE
                s = jnp.where(valid, s, NEG_BIG)
                m = jnp.max(s, axis=-1, keepdims=True)
                p = jnp.exp2(s - m).astype(BF16)
                num = _dot(p, v2)
                den = _dot(p, ones)
                outs.append(num / den)
                lse = (m + jnp.log2(den)) * LN2
                lse_tile = jnp.where(lane == 2 * pair + hh, lse, lse_tile)
            o_ref[pl.ds(a0, tq), cols] = jnp.where(low_head, outs[0], outs[1]).astype(BF16)
        lse_ref[pl.ds(a0, tq), :] = lse_tile
        return carry

    lax.fori_loop(0, length // tq, body, 0)


def _dilated(qa, ka, va, dil):
    b, n, _ = qa.shape
    length = n // dil
    view = lambda a: a.reshape(b, length, dil * WIDTH_A)
    spec = pl.BlockSpec((None, length, WIDTH_A), lambda bi, r: (bi, 0, r))
    lse_spec = pl.BlockSpec((None, length, LANES), lambda bi, r: (bi, 0, r))
    o, lse = pl.pallas_call(
        _dilated_kernel,
        grid=(b, dil),
        in_specs=[spec, spec, spec],
        out_specs=(spec, lse_spec),
        out_shape=(jax.ShapeDtypeStruct((b, length, dil * WIDTH_A), BF16),
                   jax.ShapeDtypeStruct((b, length, dil * LANES), F32)),
        compiler_params=pltpu.CompilerParams(
            dimension_semantics=("arbitrary", "arbitrary"), vmem_limit_bytes=VMEM_LIMIT),
        name=f"dilated{dil}",
    )(view(qa), view(ka), view(va))
    return o.reshape(b * n, WIDTH_A), lse.reshape(b * n, LANES)


def _outproj_kernel(x_ref, o1_ref, o2_ref, o3_ref, l1_ref, l2_ref, l3_ref, ga_ref, mb_ref,
                    w_ref, ex_ref, lg_ref, lb_ref, y_ref):
    lses = (l1_ref[...], l2_ref[...], l3_ref[...])
    mx = jnp.maximum(jnp.maximum(lses[0], lses[1]), lses[2])
    es = [jnp.exp(l - mx) for l in lses]
    inv = 1.0 / (es[0] + es[1] + es[2])
    ex = ex_ref[...]
    out_a = None
    for e, o_ref in zip(es, (o1_ref, o2_ref, o3_ref)):
        wexp = _split_dot(e * inv, ex)
        term = wexp * o_ref[...].astype(F32)
        out_a = term if out_a is None else out_a + term
    mix_a = (out_a * ga_ref[...].astype(F32)).astype(BF16)
    f = _dot(mix_a, w_ref[:WIDTH_A, :]) + _dot(mb_ref[...], w_ref[WIDTH_A:, :])
    z = DEEPNORM_ALPHA * x_ref[...] + f
    mu = jnp.mean(z, axis=-1, keepdims=True)
    zc = z - mu
    var = jnp.mean(zc * zc, axis=-1, keepdims=True)
    y_ref[...] = zc * lax.rsqrt(var + LN_EPS) * lg_ref[...] + lb_ref[...]


def _outproj(x2, o_list, lse_list, ga, mix_b, w_out_bf16, expand, ln_g, ln_b):
    t = x2.shape[0]
    tm = TOKEN_TILE
    row = lambda i: (i, 0)
    const = lambda i: (0, 0)
    wide = lambda w: pl.BlockSpec((tm, w), row)
    return pl.pallas_call(
        _outproj_kernel,
        grid=(t // tm,),
        in_specs=[
            wide(D_MODEL),
            wide(WIDTH_A), wide(WIDTH_A), wide(WIDTH_A),
            wide(LANES), wide(LANES), wide(LANES),
            wide(WIDTH_A), wide(WIDTH_B),
            pl.BlockSpec((WIDTH_A + WIDTH_B, D_MODEL), const),
            pl.BlockSpec((LANES, WIDTH_A), const),
            pl.BlockSpec((1, D_MODEL), const),
            pl.BlockSpec((1, D_MODEL), const),
        ],
        out_specs=wide(D_MODEL),
        out_shape=jax.ShapeDtypeStruct((t, D_MODEL), F32),
        compiler_params=pltpu.CompilerParams(
            dimension_semantics=("arbitrary",), vmem_limit_bytes=VMEM_LIMIT),
        name="outproj",
    )(x2, *o_list, *lse_list, ga, mix_b, w_out_bf16, expand, ln_g, ln_b)


def _constants():
    head_of_lane = np.arange(LANES) // HEAD_DIM
    bd = (head_of_lane[:, None] == head_of_lane[None, :]).astype(np.float32)
    ex = (np.arange(LANES)[:, None] == (np.arange(WIDTH_A) // HEAD_DIM)[None, :]).astype(np.float32)
    return jnp.asarray(bd, BF16), jnp.asarray(ex, BF16)


def _layer(x, w_in_bf16, w_out_bf16, tables, qn, kn, ln_g, ln_b, bd, expand):
    b, n, _ = x.shape
    qa, ka, va, ga, qb, gb, kb, vbt = _inproj(x, w_in_bf16, tables, qn, kn, bd)
    mix_b = _gqa(qb.reshape(b, n, WIDTH_B), kb, vbt, gb.reshape(b, n, WIDTH_B))
    shape3 = lambda a: a.reshape(b, n, WIDTH_A)
    o_list, lse_list = [], []
    for _, dil in DILATED_PATTERNS:
        o, lse = _dilated(shape3(qa), shape3(ka), shape3(va), dil)
        o_list.append(o)
        lse_list.append(lse)
    y = _outproj(x.reshape(b * n, D_MODEL), o_list, lse_list, ga, mix_b.reshape(b * n, WIDTH_B),
                 w_out_bf16, expand, ln_g, ln_b)
    return y.reshape(b, n, D_MODEL)


def kernel(x_prompt, x_sample, w_in, w_out, q_norm, k_norm, ln_g, ln_b):
    bd, expand = _constants()
    tables = _rope_tables(max(x_prompt.shape[1], x_sample.shape[1]))
    rep = LANES // HEAD_DIM
    y_prompt, y_sample = x_prompt, x_sample
    for i in range(w_in.shape[0]):
        w_in_bf16 = w_in[i].astype(BF16)
        w_out_bf16 = w_out[i].astype(BF16)
        qn = jnp.tile(q_norm[i].reshape(1, HEAD_DIM), (1, rep))
        kn = jnp.tile(k_norm[i].reshape(1, HEAD_DIM), (1, rep))
        args = (w_in_bf16, w_out_bf16, tables, qn, kn,
                ln_g[i].reshape(1, D_MODEL), ln_b[i].reshape(1, D_MODEL), bd, expand)
        y_prompt = _layer(y_prompt, *args)
        y_sample = _layer(y_sample, *args)
    return (y_prompt, y_sample)
```

```python
import functools
import math

import numpy as np
import jax
import jax.numpy as jnp
from jax import lax
from jax.experimental import pallas as pl
from jax.experimental.pallas import tpu as pltpu

D_MODEL = 1024
HEAD_DIM = 64
N_HEADS_A = 8
N_HEADS_B = 8
N_KV_B = 2
GROUP_B = N_HEADS_B // N_KV_B
WIDTH_A = N_HEADS_A * HEAD_DIM
WIDTH_B = N_HEADS_B * HEAD_DIM
KV_WIDTH_B = N_KV_B * HEAD_DIM
DILATED_PATTERNS = ((128, 1), (512, 4), (2048, 16))
GRID_W = 64
ROPE_THETA = 10000.0
LN_EPS = 1e-5
RMS_EPS = 1e-6
NEG_BIG = -1e30
DEPTH = 1
DEEPNORM_ALPHA = (2.0 * DEPTH) ** 0.25

OFF_QA = 0
OFF_KA = OFF_QA + WIDTH_A
OFF_VA = OFF_KA + WIDTH_A
OFF_GA = OFF_VA + WIDTH_A
OFF_QB = OFF_GA + WIDTH_A
OFF_KB = OFF_QB + WIDTH_B
OFF_VB = OFF_KB + KV_WIDTH_B
OFF_GB = OFF_VB + KV_WIDTH_B
IN_WIDTH = OFF_GB + WIDTH_B

LANES = 128
LOG2E = 1.4426950408889634
LN2 = 0.6931471805599453
SCORE_SCALE = (HEAD_DIM ** -0.5) * LOG2E

TOKEN_TILE = 512
GQA_TQ = 256
GQA_TK = 512
BF16_ROWS = 16
DIL_TQ = 128
DIL_HALF = 64
VMEM_LIMIT = 48 * 1024 * 1024

F32 = jnp.float32
BF16 = jnp.bfloat16


def _dot(a, b):
    return jnp.dot(a, b, preferred_element_type=F32)


def _dot_nt(a, b):
    return lax.dot_general(a, b, (((1,), (1,)), ((), ())), preferred_element_type=F32)


def _split_dot(x, w_bf16):
    hi = x.astype(BF16)
    lo = (x - hi.astype(F32)).astype(BF16)
    return _dot(hi, w_bf16) + _dot(lo, w_bf16)


def _rotate_half(x, first, half):
    return jnp.where(first, pltpu.roll(x, LANES - half, 1), pltpu.roll(x, half, 1))


def _inproj_kernel(x_ref, w_ref, ca_ref, sa_ref, cb_ref, sb_ref, qn_ref, kn_ref, bd_ref,
                   qa_ref, ka_ref, va_ref, ga_ref, qb_ref, gb_ref, kb_ref, vbt_ref):
    tm = x_ref.shape[0]
    xb = x_ref[...].astype(BF16)

    def proj(off, width):
        return _dot(xb, w_ref[:, off:off + width])

    lane = lax.broadcasted_iota(jnp.int32, (tm, LANES), 1)
    first_a = (lane % HEAD_DIM) < (HEAD_DIM // 2)
    first_b = (lane % (HEAD_DIM // 2)) < (HEAD_DIM // 4)
    ca, sa = ca_ref[...], sa_ref[...]
    cb, sb = cb_ref[...], sb_ref[...]
    bd = bd_ref[...]

    def rope_a(x):
        return x * ca + _rotate_half(x, first_a, HEAD_DIM // 2) * sa

    def norm_rope_b(x, gain):
        ms = _split_dot(x * x, bd) * (1.0 / HEAD_DIM)
        xn = x * lax.rsqrt(ms + RMS_EPS) * gain
        return xn * cb + _rotate_half(xn, first_b, HEAD_DIM // 4) * sb

    def silu(g):
        return g * (1.0 / (1.0 + jnp.exp(-g)))

    h = proj(OFF_QA, WIDTH_A)
    for c in range(WIDTH_A // LANES):
        qa = rope_a(h[:, c * LANES:(c + 1) * LANES]) * SCORE_SCALE
        qa_ref[:, c * LANES:(c + 1) * LANES] = qa.astype(BF16)
    h = proj(OFF_KA, WIDTH_A)
    for c in range(WIDTH_A // LANES):
        ka_ref[:, c * LANES:(c + 1) * LANES] = rope_a(h[:, c * LANES:(c + 1) * LANES]).astype(BF16)
    va_ref[...] = proj(OFF_VA, WIDTH_A).astype(BF16)
    ga_ref[...] = silu(proj(OFF_GA, WIDTH_A)).astype(BF16)

    h = proj(OFF_QB, WIDTH_B)
    qn = qn_ref[...]
    for c in range(WIDTH_B // LANES):
        qb = norm_rope_b(h[:, c * LANES:(c + 1) * LANES], qn) * SCORE_SCALE
        qb_ref[:, c * LANES:(c + 1) * LANES] = qb.astype(BF16)
    gb_ref[...] = silu(proj(OFF_GB, WIDTH_B)).astype(BF16)

    h = proj(OFF_KB, 2 * KV_WIDTH_B)
    kb = norm_rope_b(h[:, :KV_WIDTH_B], kn_ref[...])
    kb_swapped = pltpu.roll(kb, HEAD_DIM, 1)
    low = lane < HEAD_DIM
    kb_ref[0] = jnp.where(low, kb, kb_swapped).astype(BF16)
    kb_ref[1] = jnp.where(low, kb_swapped, kb).astype(BF16)
    vbt = h[:, KV_WIDTH_B:].T
    for j in range(N_KV_B):
        vbt_ref[j] = vbt[j * HEAD_DIM:(j + 1) * HEAD_DIM, :].astype(BF16)


def _rope_tables(n):
    def cos_sin(pos, dim):
        inv_freq = 1.0 / (ROPE_THETA ** (jnp.arange(0, dim, 2, dtype=F32) / dim))
        ang = pos.astype(F32)[:, None] * inv_freq[None, :]
        ang = jnp.concatenate([ang, ang], axis=-1)
        sign = jnp.concatenate([-jnp.ones((dim // 2,), F32), jnp.ones((dim // 2,), F32)])
        return jnp.cos(ang), jnp.sin(ang) * sign[None, :]

    pos = jnp.arange(n, dtype=jnp.int32)
    ca, sa = cos_sin(pos, HEAD_DIM)
    row = pos // GRID_W
    col = pos % GRID_W
    cr, sr = cos_sin(row, HEAD_DIM // 2)
    cc, sc = cos_sin(col, HEAD_DIM // 2)
    cb = jnp.concatenate([cr, cc], axis=-1)
    sb = jnp.concatenate([sr, sc], axis=-1)
    rep = LANES // HEAD_DIM
    return tuple(jnp.tile(t, (1, rep)) for t in (ca, sa, cb, sb))


def _inproj(x, w_bf16, tables, qn, kn, bd):
    b, n, _ = x.shape
    tm = TOKEN_TILE
    nt = n // tm
    t = b * n
    x2 = x.reshape(t, D_MODEL)
    row = lambda i: (i, 0)
    pos = lambda i: (i % nt, 0)
    const = lambda i: (0, 0)
    tab_spec = pl.BlockSpec((tm, LANES), pos)
    wide = lambda w: pl.BlockSpec((tm, w), row)
    out_shapes = (
        jax.ShapeDtypeStruct((t, WIDTH_A), BF16),
        jax.ShapeDtypeStruct((t, WIDTH_A), BF16),
        jax.ShapeDtypeStruct((t, WIDTH_A), BF16),
        jax.ShapeDtypeStruct((t, WIDTH_A), BF16),
        jax.ShapeDtypeStruct((t, WIDTH_B), BF16),
        jax.ShapeDtypeStruct((t, WIDTH_B), BF16),
        jax.ShapeDtypeStruct((b, N_KV_B, n, LANES), BF16),
        jax.ShapeDtypeStruct((b, N_KV_B, HEAD_DIM, n), BF16),
    )
    out_specs = (
        wide(WIDTH_A), wide(WIDTH_A), wide(WIDTH_A), wide(WIDTH_A), wide(WIDTH_B), wide(WIDTH_B),
        pl.BlockSpec((None, N_KV_B, tm, LANES), lambda i: (i // nt, 0, i % nt, 0)),
        pl.BlockSpec((None, N_KV_B, HEAD_DIM, tm), lambda i: (i // nt, 0, 0, i % nt)),
    )
    in_specs = [
        pl.BlockSpec((tm, D_MODEL), row),
        pl.BlockSpec((D_MODEL, IN_WIDTH), const),
        tab_spec, tab_spec, tab_spec, tab_spec,
        pl.BlockSpec((1, LANES), const),
        pl.BlockSpec((1, LANES), const),
        pl.BlockSpec((LANES, LANES), const),
    ]
    return pl.pallas_call(
        _inproj_kernel,
        grid=(t // tm,),
        in_specs=in_specs,
        out_specs=out_specs,
        out_shape=out_shapes,
        compiler_params=pltpu.CompilerParams(
            dimension_semantics=("arbitrary",), vmem_limit_bytes=VMEM_LIMIT),
        name="inproj",
    )(x2, w_bf16, *tables, qn, kn, bd)


def _gqa_kernel(q_ref, k_ref, vt_ref, g_ref, o_ref, qs_ref, s0_ref, s1_ref, m_ref, acc_ref):
    tq = q_ref.shape[0]
    n = k_ref.shape[0]
    tk = GQA_TK
    nc = n // tk
    low = lax.broadcasted_iota(jnp.int32, (tq, LANES), 1) < HEAD_DIM
    for h in range(GROUP_B):
        pair = q_ref[:, (h // 2) * LANES:(h // 2 + 1) * LANES]
        mine = low if h % 2 == 0 else jnp.logical_not(low)
        qs_ref[h * tq:(h + 1) * tq, :] = jnp.where(mine, pair, jnp.zeros_like(pair))
    m_ref[...] = jnp.full(m_ref.shape, NEG_BIG, F32)
    acc_ref[...] = jnp.zeros(acc_ref.shape, F32)
    ones = jnp.ones((BF16_ROWS, tk), BF16)

    def scores(c):
        off = pl.multiple_of(c * tk, tk)
        return _dot_nt(k_ref[pl.ds(off, tk), :], qs_ref[...])

    def update(s, c):
        off = pl.multiple_of(c * tk, tk)
        m_prev = m_ref[...]
        m_new = jnp.maximum(m_prev, jnp.max(s, axis=0, keepdims=True))
        alpha = jnp.exp2(m_prev - m_new)
        p = jnp.exp2(s - m_new).astype(BF16)
        vt = jnp.concatenate([vt_ref[:, pl.ds(off, tk)], ones], axis=0)
        acc_ref[...] = alpha * acc_ref[...] + _dot(vt, p)
        m_ref[...] = m_new

    s0_ref[...] = scores(0)

    def body(cc, carry):
        c = 2 * cc
        s1_ref[...] = scores(c + 1)
        update(s0_ref[...], c)
        s0_ref[...] = scores(c + 2)
        update(s1_ref[...], c + 1)
        return carry

    lax.fori_loop(0, nc // 2 - 1, body, 0)
    s1_ref[...] = scores(nc - 1)
    update(s0_ref[...], nc - 2)
    update(s1_ref[...], nc - 1)

    acc = acc_ref[...]
    o_t = acc[:HEAD_DIM] * (1.0 / acc[HEAD_DIM:HEAD_DIM + 1])
    for pr in range(GROUP_B // 2):
        pair_t = jnp.concatenate([o_t[:, (2 * pr) * tq:(2 * pr + 1) * tq],
                                  o_t[:, (2 * pr + 1) * tq:(2 * pr + 2) * tq]], axis=0)
        gate = g_ref[:, pr * LANES:(pr + 1) * LANES].astype(F32)
        o_ref[:, pr * LANES:(pr + 1) * LANES] = (pair_t.T * gate).astype(BF16)


def _gqa(qb, kb, vbt, gb):
    b, n, _ = qb.shape
    tq = GQA_TQ
    gw = GROUP_B * HEAD_DIM
    assert (n // GQA_TK) % 2 == 0
    tile = pl.BlockSpec((None, tq, gw), lambda bi, j, i: (bi, i, j))
    return pl.pallas_call(
        _gqa_kernel,
        grid=(b, N_KV_B, n // tq),
        in_specs=[
            tile,
            pl.BlockSpec((None, None, n, LANES), lambda bi, j, i: (bi, j, 0, 0)),
            pl.BlockSpec((None, None, HEAD_DIM, n), lambda bi, j, i: (bi, j, 0, 0)),
            tile,
        ],
        out_specs=tile,
        out_shape=jax.ShapeDtypeStruct((b, n, WIDTH_B), BF16),
        scratch_shapes=[
            pltpu.VMEM((GROUP_B * tq, LANES), BF16),
            pltpu.VMEM((GQA_TK, GROUP_B * tq), F32),
            pltpu.VMEM((GQA_TK, GROUP_B * tq), F32),
            pltpu.VMEM((1, GROUP_B * tq), F32),
            pltpu.VMEM((HEAD_DIM + BF16_ROWS, GROUP_B * tq), F32),
        ],
        compiler_params=pltpu.CompilerParams(
            dimension_semantics=("arbitrary", "arbitrary", "arbitrary"),
            vmem_limit_bytes=VMEM_LIMIT),
        name="gqa",
    )(qb, kb, vbt, gb)


def _dilated_kernel(q_ref, k_ref, v_ref, bias_ref, o_ref, lse_ref, s0_ref, s1_ref):
    length = q_ref.shape[0]
    tq = DIL_TQ
    nb = length // tq
    win = min(tq + 2 * DIL_HALF, length)
    n_pairs = N_HEADS_A // 2
    low_lane = lax.broadcasted_iota(jnp.int32, (tq, LANES), 1) < HEAD_DIM
    low_row = lax.broadcasted_iota(jnp.int32, (LANES, tq), 0) < HEAD_DIM
    row8 = lax.broadcasted_iota(jnp.int32, (8, tq), 0)

    def window(i):
        a0 = pl.multiple_of(i * tq, tq)
        ws = pl.multiple_of(jnp.clip(a0 - DIL_HALF, 0, length - win), DIL_HALF)
        return a0, ws

    def scores(i, s_ref):
        a0, ws = window(i)
        for pair in range(n_pairs):
            cols = slice(pair * LANES, (pair + 1) * LANES)
            q2 = q_ref[pl.ds(a0, tq), cols]
            k2 = k_ref[pl.ds(ws, win), cols]
            zero = jnp.zeros_like(q2)
            qs = jnp.concatenate([jnp.where(low_lane, q2, zero),
                                  jnp.where(low_lane, zero, q2)], axis=0)
            s_ref[pair] = _dot_nt(k2, qs)

    def attend(i, s_ref):
        a0, ws = window(i)
        bias = bias_ref[jnp.where(i == 0, 0, jnp.where(i == nb - 1, 2, 1))]
        lse8 = jnp.zeros((8, tq), F32)
        for pair in range(n_pairs):
            cols = slice(pair * LANES, (pair + 1) * LANES)
            v2 = v_ref[pl.ds(ws, win), cols]
            s = s_ref[pair] + bias
            m = jnp.max(s, axis=0, keepdims=True)
            p = jnp.exp2(s - m)
            den = jnp.sum(p, axis=0, keepdims=True)
            o_t = lax.dot_general(v2, p.astype(BF16), (((0,), (0,)), ((), ())),
                                  preferred_element_type=F32)
            o_t = o_t * (1.0 / den)
            pair_t = jnp.where(low_row, o_t[:, :tq], o_t[:, tq:])
            o_ref[pl.ds(a0, tq), cols] = pair_t.T.astype(BF16)
            lse = (m + jnp.log2(den)) * LN2
            lse8 = jnp.where(row8 == 2 * pair, lse[:, :tq], lse8)
            lse8 = jnp.where(row8 == 2 * pair + 1, lse[:, tq:], lse8)
        lse_t = jnp.concatenate([lse8, jnp.zeros((LANES - 8, tq), F32)], axis=0)
        lse_ref[pl.ds(a0, tq), :] = lse_t.T

    scores(0, s0_ref)
    if nb > 1:
        def body(ii, carry):
            i = 2 * ii
            scores(i + 1, s1_ref)
            attend(i, s0_ref)
            scores(i + 2, s0_ref)
            attend(i + 1, s1_ref)
            return carry

        lax.fori_loop(0, nb // 2 - 1, body, 0)
        scores(nb - 1, s1_ref)
        attend(nb - 2, s0_ref)
        attend(nb - 1, s1_ref)
    else:
        attend(0, s0_ref)


def _dilated(qa, ka, va, dil):
    b, n, _ = qa.shape
    length = n // dil
    nb = length // DIL_TQ
    assert nb == 1 or nb % 2 == 0
    win = min(DIL_TQ + 2 * DIL_HALF, length)
    s_scratch = pltpu.VMEM((N_HEADS_A // 2, win, 2 * DIL_TQ), F32)
    rel = np.arange(win)[:, None] - (np.arange(2 * DIL_TQ) % DIL_TQ)[None, :]
    bias = np.stack([np.where(np.abs(rel + off) <= DIL_HALF, 0.0, NEG_BIG)
                     for off in (0, -DIL_HALF, DIL_TQ - win)]).astype(np.float32)
    bias_spec = pl.BlockSpec(bias.shape, lambda bi, r: (0, 0, 0))
    view = lambda a: a.reshape(b, length, dil * WIDTH_A)
    spec = pl.BlockSpec((None, length, WIDTH_A), lambda bi, r: (bi, 0, r))
    lse_spec = pl.BlockSpec((None, length, LANES), lambda bi, r: (bi, 0, r))
    o, lse = pl.pallas_call(
        _dilated_kernel,
        grid=(b, dil),
        in_specs=[spec, spec, spec, bias_spec],
        out_specs=(spec, lse_spec),
        out_shape=(jax.ShapeDtypeStruct((b, length, dil * WIDTH_A), BF16),
                   jax.ShapeDtypeStruct((b, length, dil * LANES), F32)),
        scratch_shapes=[s_scratch, s_scratch],
        compiler_params=pltpu.CompilerParams(
            dimension_semantics=("arbitrary", "arbitrary"), vmem_limit_bytes=VMEM_LIMIT),
        name=f"dilated{dil}",
    )(view(qa), view(ka), view(va), jnp.asarray(bias))
    return o.reshape(b * n, WIDTH_A), lse.reshape(b * n, LANES)


def _outproj_kernel(x_ref, o1_ref, o2_ref, o3_ref, l1_ref, l2_ref, l3_ref, ga_ref, mb_ref,
                    w_ref, ex_ref, lg_ref, lb_ref, y_ref):
    lses = (l1_ref[...], l2_ref[...], l3_ref[...])
    mx = jnp.maximum(jnp.maximum(lses[0], lses[1]), lses[2])
    es = [jnp.exp(l - mx) for l in lses]
    inv = 1.0 / (es[0] + es[1] + es[2])
    ex = ex_ref[...]
    out_a = None
    for e, o_ref in zip(es, (o1_ref, o2_ref, o3_ref)):
        wexp = _split_dot(e * inv, ex)
        term = wexp * o_ref[...].astype(F32)
        out_a = term if out_a is None else out_a + term
    mix_a = (out_a * ga_ref[...].astype(F32)).astype(BF16)
    f = _dot(mix_a, w_ref[:WIDTH_A, :]) + _dot(mb_ref[...], w_ref[WIDTH_A:, :])
    z = DEEPNORM_ALPHA * x_ref[...] + f
    mu = jnp.mean(z, axis=-1, keepdims=True)
    zc = z - mu
    var = jnp.mean(zc * zc, axis=-1, keepdims=True)
    y_ref[...] = zc * lax.rsqrt(var + LN_EPS) * lg_ref[...] + lb_ref[...]


def _outproj(x2, o_list, lse_list, ga, mix_b, w_out_bf16, expand, ln_g, ln_b):
    t = x2.shape[0]
    tm = TOKEN_TILE
    row = lambda i: (i, 0)
    const = lambda i: (0, 0)
    wide = lambda w: pl.BlockSpec((tm, w), row)
    return pl.pallas_call(
        _outproj_kernel,
        grid=(t // tm,),
        in_specs=[
            wide(D_MODEL),
            wide(WIDTH_A), wide(WIDTH_A), wide(WIDTH_A),
            wide(LANES), wide(LANES), wide(LANES),
            wide(WIDTH_A), wide(WIDTH_B),
            pl.BlockSpec((WIDTH_A + WIDTH_B, D_MODEL), const),
            pl.BlockSpec((LANES, WIDTH_A), const),
            pl.BlockSpec((1, D_MODEL), const),
            pl.BlockSpec((1, D_MODEL), const),
        ],
        out_specs=wide(D_MODEL),
        out_shape=jax.ShapeDtypeStruct((t, D_MODEL), F32),
        compiler_params=pltpu.CompilerParams(
            dimension_semantics=("arbitrary",), vmem_limit_bytes=VMEM_LIMIT),
        name="outproj",
    )(x2, *o_list, *lse_list, ga, mix_b, w_out_bf16, expand, ln_g, ln_b)


def _constants():
    head_of_lane = np.arange(LANES) // HEAD_DIM
    bd = (head_of_lane[:, None] == head_of_lane[None, :]).astype(np.float32)
    ex = (np.arange(LANES)[:, None] == (np.arange(WIDTH_A) // HEAD_DIM)[None, :]).astype(np.float32)
    return jnp.asarray(bd, BF16), jnp.asarray(ex, BF16)


def _layer(x, w_in_bf16, w_out_bf16, tables, qn, kn, ln_g, ln_b, bd, expand):
    b, n, _ = x.shape
    qa, ka, va, ga, qb, gb, kb, vbt = _inproj(x, w_in_bf16, tables, qn, kn, bd)
    mix_b = _gqa(qb.reshape(b, n, WIDTH_B), kb, vbt, gb.reshape(b, n, WIDTH_B))
    shape3 = lambda a: a.reshape(b, n, WIDTH_A)
    o_list, lse_list = [], []
    for _, dil in DILATED_PATTERNS:
        o, lse = _dilated(shape3(qa), shape3(ka), shape3(va), dil)
        o_list.append(o)
        lse_list.append(lse)
    y = _outproj(x.reshape(b * n, D_MODEL), o_list, lse_list, ga, mix_b.reshape(b * n, WIDTH_B),
                 w_out_bf16, expand, ln_g, ln_b)
    return y.reshape(b, n, D_MODEL)


def kernel(x_prompt, x_sample, w_in, w_out, q_norm, k_norm, ln_g, ln_b):
    bd, expand = _constants()
    tables = _rope_tables(max(x_prompt.shape[1], x_sample.shape[1]))
    rep = LANES // HEAD_DIM
    y_prompt, y_sample = x_prompt, x_sample
    for i in range(w_in.shape[0]):
        w_in_bf16 = w_in[i].astype(BF16)
        w_out_bf16 = w_out[i].astype(BF16)
        qn = jnp.tile(q_norm[i].reshape(1, HEAD_DIM), (1, rep))
        kn = jnp.tile(k_norm[i].reshape(1, HEAD_DIM), (1, rep))
        args = (w_in_bf16, w_out_bf16, tables, qn, kn,
                ln_g[i].reshape(1, D_MODEL), ln_b[i].reshape(1, D_MODEL), bd, expand)
        y_prompt = _layer(y_prompt, *args)
        y_sample = _layer(y_sample, *args)
    return (y_prompt, y_sample)
```

```python
import functools
import math

import numpy as np
import jax
import jax.numpy as jnp
from jax import lax
from jax.experimental import pallas as pl
from jax.experimental.pallas import tpu as pltpu

D_MODEL = 1024
HEAD_DIM = 64
N_HEADS_A = 8
N_HEADS_B = 8
N_KV_B = 2
GROUP_B = N_HEADS_B // N_KV_B
WIDTH_A = N_HEADS_A * HEAD_DIM
WIDTH_B = N_HEADS_B * HEAD_DIM
KV_WIDTH_B = N_KV_B * HEAD_DIM
DILATED_PATTERNS = ((128, 1), (512, 4), (2048, 16))
GRID_W = 64
ROPE_THETA = 10000.0
LN_EPS = 1e-5
RMS_EPS = 1e-6
NEG_BIG = -1e30
DEPTH = 1
DEEPNORM_ALPHA = (2.0 * DEPTH) ** 0.25

OFF_QA = 0
OFF_KA = OFF_QA + WIDTH_A
OFF_VA = OFF_KA + WIDTH_A
OFF_GA = OFF_VA + WIDTH_A
OFF_QB = OFF_GA + WIDTH_A
OFF_KB = OFF_QB + WIDTH_B
OFF_VB = OFF_KB + KV_WIDTH_B
OFF_GB = OFF_VB + KV_WIDTH_B
IN_WIDTH = OFF_GB + WIDTH_B

LANES = 128
LOG2E = 1.4426950408889634
LN2 = 0.6931471805599453
SCORE_SCALE = (HEAD_DIM ** -0.5) * LOG2E

TOKEN_TILE = 512
GQA_TQ = 256
GQA_TK = 512
BF16_ROWS = 16
DIL_TQ = 128
DIL_HALF = 64
VMEM_LIMIT = 48 * 1024 * 1024

F32 = jnp.float32
BF16 = jnp.bfloat16


def _dot(a, b):
    return jnp.dot(a, b, preferred_element_type=F32)


def _dot_nt(a, b):
    return lax.dot_general(a, b, (((1,), (1,)), ((), ())), preferred_element_type=F32)


def _split_dot(x, w_bf16):
    hi = x.astype(BF16)
    lo = (x - hi.astype(F32)).astype(BF16)
    return _dot(hi, w_bf16) + _dot(lo, w_bf16)


def _rotate_half(x, first, half):
    return jnp.where(first, pltpu.roll(x, LANES - half, 1), pltpu.roll(x, half, 1))


def _inproj_kernel(x_ref, w_ref, ca_ref, sa_ref, cb_ref, sb_ref, qn_ref, kn_ref, bd_ref,
                   qa_ref, ka_ref, va_ref, qa4_ref, ka4_ref, va4_ref, qa16_ref, ka16_ref, va16_ref,
                   ga_ref, qb_ref, gb_ref, kb_ref, vbt_ref, slab_ref):
    tm = x_ref.shape[0]
    xb = x_ref[...].astype(BF16)

    def proj(off, width):
        return _dot(xb, w_ref[:, off:off + width])

    lane = lax.broadcasted_iota(jnp.int32, (tm, LANES), 1)
    first_a = (lane % HEAD_DIM) < (HEAD_DIM // 2)
    first_b = (lane % (HEAD_DIM // 2)) < (HEAD_DIM // 4)
    ca, sa = ca_ref[...], sa_ref[...]
    cb, sb = cb_ref[...], sb_ref[...]
    bd = bd_ref[...]

    def rope_a(x):
        return x * ca + _rotate_half(x, first_a, HEAD_DIM // 2) * sa

    def norm_rope_b(x, gain):
        ms = _split_dot(x * x, bd) * (1.0 / HEAD_DIM)
        xn = x * lax.rsqrt(ms + RMS_EPS) * gain
        return xn * cb + _rotate_half(xn, first_b, HEAD_DIM // 4) * sb

    def silu(g):
        return g * (1.0 / (1.0 + jnp.exp(-g)))

    n_slabs = WIDTH_A // LANES
    finish = (lambda x: rope_a(x) * SCORE_SCALE, rope_a, lambda x: x)
    for which, (off, nat_ref) in enumerate(((OFF_QA, qa_ref), (OFF_KA, ka_ref), (OFF_VA, va_ref))):
        h = proj(off, WIDTH_A)
        for c in range(n_slabs):
            cols = slice(c * LANES, (c + 1) * LANES)
            val = finish[which](h[:, cols])
            nat_ref[:, cols] = val.astype(BF16)
            slab_ref[which * n_slabs + c] = val
    for dil, refs in ((4, (qa4_ref, ka4_ref, va4_ref)), (16, (qa16_ref, ka16_ref, va16_ref))):
        rows = tm // dil
        for which in range(3):
            for c in range(n_slabs):
                cols = slice(c * LANES, (c + 1) * LANES)
                for r in range(dil):
                    part = slab_ref[which * n_slabs + c, pl.ds(r, rows, stride=dil), :]
                    refs[which][r, :, cols] = part.astype(BF16)
    ga_ref[...] = silu(proj(OFF_GA, WIDTH_A)).astype(BF16)

    h = proj(OFF_QB, WIDTH_B)
    qn = qn_ref[...]
    for c in range(WIDTH_B // LANES):
        qb = norm_rope_b(h[:, c * LANES:(c + 1) * LANES], qn) * SCORE_SCALE
        qb_ref[:, c * LANES:(c + 1) * LANES] = qb.astype(BF16)
    gb_ref[...] = silu(proj(OFF_GB, WIDTH_B)).astype(BF16)

    h = proj(OFF_KB, 2 * KV_WIDTH_B)
    kb = norm_rope_b(h[:, :KV_WIDTH_B], kn_ref[...])
    kb_swapped = pltpu.roll(kb, HEAD_DIM, 1)
    low = lane < HEAD_DIM
    kb_ref[0] = jnp.where(low, kb, kb_swapped).astype(BF16)
    kb_ref[1] = jnp.where(low, kb_swapped, kb).astype(BF16)
    vbt = h[:, KV_WIDTH_B:].T
    for j in range(N_KV_B):
        vbt_ref[j] = vbt[j * HEAD_DIM:(j + 1) * HEAD_DIM, :].astype(BF16)


def _rope_tables(n):
    def cos_sin(pos, dim):
        inv_freq = 1.0 / (ROPE_THETA ** (jnp.arange(0, dim, 2, dtype=F32) / dim))
        ang = pos.astype(F32)[:, None] * inv_freq[None, :]
        ang = jnp.concatenate([ang, ang], axis=-1)
        sign = jnp.concatenate([-jnp.ones((dim // 2,), F32), jnp.ones((dim // 2,), F32)])
        return jnp.cos(ang), jnp.sin(ang) * sign[None, :]

    pos = jnp.arange(n, dtype=jnp.int32)
    ca, sa = cos_sin(pos, HEAD_DIM)
    row = pos // GRID_W
    col = pos % GRID_W
    cr, sr = cos_sin(row, HEAD_DIM // 2)
    cc, sc = cos_sin(col, HEAD_DIM // 2)
    cb = jnp.concatenate([cr, cc], axis=-1)
    sb = jnp.concatenate([sr, sc], axis=-1)
    rep = LANES // HEAD_DIM
    return tuple(jnp.tile(t, (1, rep)) for t in (ca, sa, cb, sb))


def _inproj(x, w_bf16, tables, qn, kn, bd):
    b, n, _ = x.shape
    tm = TOKEN_TILE
    nt = n // tm
    t = b * n
    x2 = x.reshape(t, D_MODEL)
    row = lambda i: (i, 0)
    pos = lambda i: (i % nt, 0)
    const = lambda i: (0, 0)
    tab_spec = pl.BlockSpec((tm, LANES), pos)
    wide = lambda w: pl.BlockSpec((tm, w), row)
    def by_residue(dil):
        shape = jax.ShapeDtypeStruct((b, dil, n // dil, WIDTH_A), BF16)
        spec = pl.BlockSpec((None, dil, tm // dil, WIDTH_A), lambda i: (i // nt, 0, i % nt, 0))
        return [shape] * 3, [spec] * 3

    shapes4, specs4 = by_residue(4)
    shapes16, specs16 = by_residue(16)
    out_shapes = (
        jax.ShapeDtypeStruct((t, WIDTH_A), BF16),
        jax.ShapeDtypeStruct((t, WIDTH_A), BF16),
        jax.ShapeDtypeStruct((t, WIDTH_A), BF16),
        *shapes4, *shapes16,
        jax.ShapeDtypeStruct((t, WIDTH_A), BF16),
        jax.ShapeDtypeStruct((t, WIDTH_B), BF16),
        jax.ShapeDtypeStruct((t, WIDTH_B), BF16),
        jax.ShapeDtypeStruct((b, N_KV_B, n, LANES), BF16),
        jax.ShapeDtypeStruct((b, N_KV_B, HEAD_DIM, n), BF16),
    )
    out_specs = (
        wide(WIDTH_A), wide(WIDTH_A), wide(WIDTH_A), *specs4, *specs16,
        wide(WIDTH_A), wide(WIDTH_B), wide(WIDTH_B),
        pl.BlockSpec((None, N_KV_B, tm, LANES), lambda i: (i // nt, 0, i % nt, 0)),
        pl.BlockSpec((None, N_KV_B, HEAD_DIM, tm), lambda i: (i // nt, 0, 0, i % nt)),
    )
    in_specs = [
        pl.BlockSpec((tm, D_MODEL), row),
        pl.BlockSpec((D_MODEL, IN_WIDTH), const),
        tab_spec, tab_spec, tab_spec, tab_spec,
        pl.BlockSpec((1, LANES), const),
        pl.BlockSpec((1, LANES), const),
        pl.BlockSpec((LANES, LANES), const),
    ]
    return pl.pallas_call(
        _inproj_kernel,
        grid=(t // tm,),
        in_specs=in_specs,
        out_specs=out_specs,
        out_shape=out_shapes,
        scratch_shapes=[pltpu.VMEM((3 * WIDTH_A // LANES, tm, LANES), F32)],
        compiler_params=pltpu.CompilerParams(
            dimension_semantics=("arbitrary",), vmem_limit_bytes=VMEM_LIMIT),
        name="inproj",
    )(x2, w_bf16, *tables, qn, kn, bd)


def _gqa_kernel(q_ref, k_ref, vt_ref, g_ref, o_ref, qs_ref, s0_ref, s1_ref, m_ref, acc_ref):
    tq = q_ref.shape[0]
    n = k_ref.shape[0]
    tk = GQA_TK
    nc = n // tk
    low = lax.broadcasted_iota(jnp.int32, (tq, LANES), 1) < HEAD_DIM
    for h in range(GROUP_B):
        pair = q_ref[:, (h // 2) * LANES:(h // 2 + 1) * LANES]
        mine = low if h % 2 == 0 else jnp.logical_not(low)
        qs_ref[h * tq:(h + 1) * tq, :] = jnp.where(mine, pair, jnp.zeros_like(pair))
    m_ref[...] = jnp.full(m_ref.shape, NEG_BIG, F32)
    acc_ref[...] = jnp.zeros(acc_ref.shape, F32)
    ones = jnp.ones((BF16_ROWS, tk), BF16)

    def scores(c):
        off = pl.multiple_of(c * tk, tk)
        return _dot_nt(k_ref[pl.ds(off, tk), :], qs_ref[...])

    def update(s, c):
        off = pl.multiple_of(c * tk, tk)
        m_prev = m_ref[...]
        m_new = jnp.maximum(m_prev, jnp.max(s, axis=0, keepdims=True))
        alpha = jnp.exp2(m_prev - m_new)
        p = jnp.exp2(s - m_new).astype(BF16)
        vt = jnp.concatenate([vt_ref[:, pl.ds(off, tk)], ones], axis=0)
        acc_ref[...] = alpha * acc_ref[...] + _dot(vt, p)
        m_ref[...] = m_new

    s0_ref[...] = scores(0)

    def body(cc, carry):
        c = 2 * cc
        s1_ref[...] = scores(c + 1)
        update(s0_ref[...], c)
        s0_ref[...] = scores(c + 2)
        update(s1_ref[...], c + 1)
        return carry

    lax.fori_loop(0, nc // 2 - 1, body, 0)
    s1_ref[...] = scores(nc - 1)
    update(s0_ref[...], nc - 2)
    update(s1_ref[...], nc - 1)

    acc = acc_ref[...]
    o_t = acc[:HEAD_DIM] * (1.0 / acc[HEAD_DIM:HEAD_DIM + 1])
    for pr in range(GROUP_B // 2):
        pair_t = jnp.concatenate([o_t[:, (2 * pr) * tq:(2 * pr + 1) * tq],
                                  o_t[:, (2 * pr + 1) * tq:(2 * pr + 2) * tq]], axis=0)
        gate = g_ref[:, pr * LANES:(pr + 1) * LANES].astype(F32)
        o_ref[:, pr * LANES:(pr + 1) * LANES] = (pair_t.T * gate).astype(BF16)


def _gqa(qb, kb, vbt, gb):
    b, n, _ = qb.shape
    tq = GQA_TQ
    gw = GROUP_B * HEAD_DIM
    assert (n // GQA_TK) % 2 == 0
    tile = pl.BlockSpec((None, tq, gw), lambda bi, j, i: (bi, i, j))
    return pl.pallas_call(
        _gqa_kernel,
        grid=(b, N_KV_B, n // tq),
        in_specs=[
            tile,
            pl.BlockSpec((None, None, n, LANES), lambda bi, j, i: (bi, j, 0, 0)),
            pl.BlockSpec((None, None, HEAD_DIM, n), lambda bi, j, i: (bi, j, 0, 0)),
            tile,
        ],
        out_specs=tile,
        out_shape=jax.ShapeDtypeStruct((b, n, WIDTH_B), BF16),
        scratch_shapes=[
            pltpu.VMEM((GROUP_B * tq, LANES), BF16),
            pltpu.VMEM((GQA_TK, GROUP_B * tq), F32),
            pltpu.VMEM((GQA_TK, GROUP_B * tq), F32),
            pltpu.VMEM((1, GROUP_B * tq), F32),
            pltpu.VMEM((HEAD_DIM + BF16_ROWS, GROUP_B * tq), F32),
        ],
        compiler_params=pltpu.CompilerParams(
            dimension_semantics=("arbitrary", "arbitrary", "arbitrary"),
            vmem_limit_bytes=VMEM_LIMIT),
        name="gqa",
    )(qb, kb, vbt, gb)


def _dilated_kernel(q_ref, k_ref, v_ref, bias_ref, o_ref, lse_ref, s0_ref, s1_ref):
    dil, length = q_ref.shape[0], q_ref.shape[1]
    tq = DIL_TQ
    nb = length // tq
    steps = dil * nb
    win = min(tq + 2 * DIL_HALF, length)
    n_pairs = N_HEADS_A // 2
    low_lane = lax.broadcasted_iota(jnp.int32, (tq, LANES), 1) < HEAD_DIM
    low_row = lax.broadcasted_iota(jnp.int32, (LANES, tq), 0) < HEAD_DIM
    row8 = lax.broadcasted_iota(jnp.int32, (8, tq), 0)

    def window(t):
        r, i = t // nb, t % nb
        if isinstance(t, int):
            a0 = i * tq
            ws = min(max(a0 - DIL_HALF, 0), length - win)
            table = 0 if i == 0 else (2 if i == nb - 1 else 1)
        else:
            a0 = pl.multiple_of(i * tq, tq)
            ws = pl.multiple_of(jnp.clip(a0 - DIL_HALF, 0, length - win), DIL_HALF)
            table = jnp.where(i == 0, 0, jnp.where(i == nb - 1, 2, 1))
        return r, table, a0, ws

    def scores(t, s_ref):
        r, _, a0, ws = window(t)
        for pair in range(n_pairs):
            cols = slice(pair * LANES, (pair + 1) * LANES)
            q2 = q_ref[r, pl.ds(a0, tq), cols]
            k2 = k_ref[r, pl.ds(ws, win), cols]
            zero = jnp.zeros_like(q2)
            qs = jnp.concatenate([jnp.where(low_lane, q2, zero),
                                  jnp.where(low_lane, zero, q2)], axis=0)
            s_ref[pair] = _dot_nt(k2, qs)

    def attend(t, s_ref):
        r, table, a0, ws = window(t)
        bias = bias_ref[table]
        lse8 = jnp.zeros((8, tq), F32)
        for pair in range(n_pairs):
            cols = slice(pair * LANES, (pair + 1) * LANES)
            v2 = v_ref[r, pl.ds(ws, win), cols]
            s = s_ref[pair] + bias
            m = jnp.max(s, axis=0, keepdims=True)
            p = jnp.exp2(s - m)
            den = jnp.sum(p, axis=0, keepdims=True)
            o_t = lax.dot_general(v2, p.astype(BF16), (((0,), (0,)), ((), ())),
                                  preferred_element_type=F32)
            o_t = o_t * (1.0 / den)
            pair_t = jnp.where(low_row, o_t[:, :tq], o_t[:, tq:])
            o_ref[r, pl.ds(a0, tq), cols] = pair_t.T.astype(BF16)
            lse = (m + jnp.log2(den)) * LN2
            lse8 = jnp.where(row8 == 2 * pair, lse[:, :tq], lse8)
            lse8 = jnp.where(row8 == 2 * pair + 1, lse[:, tq:], lse8)
        lse_t = jnp.concatenate([lse8, jnp.zeros((LANES - 8, tq), F32)], axis=0)
        lse_ref[r, pl.ds(a0, tq), :] = lse_t.T

    scores(0, s0_ref)

    def body(tt, carry):
        t = 2 * tt
        scores(t + 1, s1_ref)
        attend(t, s0_ref)
        scores(t + 2, s0_ref)
        attend(t + 1, s1_ref)
        return carry

    lax.fori_loop(0, steps // 2 - 1, body, 0)
    scores(steps - 1, s1_ref)
    attend(steps - 2, s0_ref)
    attend(steps - 1, s1_ref)


def _dilated(qa, ka, va):
    b, dil, length, _ = qa.shape
    assert (dil * (length // DIL_TQ)) % 2 == 0
    win = min(DIL_TQ + 2 * DIL_HALF, length)
    s_scratch = pltpu.VMEM((N_HEADS_A // 2, win, 2 * DIL_TQ), F32)
    rel = np.arange(win)[:, None] - (np.arange(2 * DIL_TQ) % DIL_TQ)[None, :]
    bias = np.stack([np.where(np.abs(rel + off) <= DIL_HALF, 0.0, NEG_BIG)
                     for off in (0, -DIL_HALF, DIL_TQ - win)]).astype(np.float32)
    bias_spec = pl.BlockSpec(bias.shape, lambda bi: (0, 0, 0))
    spec = pl.BlockSpec((None, dil, length, WIDTH_A), lambda bi: (bi, 0, 0, 0))
    lse_spec = pl.BlockSpec((None, dil, length, LANES), lambda bi: (bi, 0, 0, 0))
    return pl.pallas_call(
        _dilated_kernel,
        grid=(b,),
        in_specs=[spec, spec, spec, bias_spec],
        out_specs=(spec, lse_spec),
        out_shape=(jax.ShapeDtypeStruct((b, dil, length, WIDTH_A), BF16),
                   jax.ShapeDtypeStruct((b, dil, length, LANES), F32)),
        scratch_shapes=[s_scratch, s_scratch],
        compiler_params=pltpu.CompilerParams(
            dimension_semantics=("arbitrary",), vmem_limit_bytes=VMEM_LIMIT),
        name=f"dilated{dil}",
    )(qa, ka, va, jnp.asarray(bias))


def _outproj_kernel(x_ref, o1_ref, o2_ref, o3_ref, l1_ref, l2_ref, l3_ref, ga_ref, mb_ref,
                    w_ref, ex_ref, lg_ref, lb_ref, y_ref, o_scr, l_scr):
    tm = x_ref.shape[0]
    n_slabs = WIDTH_A // LANES

    def token_order(o_ref, l_ref, slot):
        dil = o_ref.shape[0]
        if dil == 1:
            return o_ref[0].astype(F32), l_ref[0]
        rows = tm // dil
        for r in range(dil):
            for c in range(n_slabs):
                part = o_ref[r, :, c * LANES:(c + 1) * LANES].astype(F32)
                o_scr[slot * n_slabs + c, pl.ds(r, rows, stride=dil), :] = part
            l_scr[slot, pl.ds(r, rows, stride=dil), :] = l_ref[r]
        o = jnp.concatenate([o_scr[slot * n_slabs + c] for c in range(n_slabs)], axis=1)
        return o, l_scr[slot]

    outs, lses = zip(*(token_order(o_ref, l_ref, slot) for slot, (o_ref, l_ref) in
                       enumerate(((o2_ref, l2_ref), (o3_ref, l3_ref), (o1_ref, l1_ref)))))
    mx = jnp.maximum(jnp.maximum(lses[0], lses[1]), lses[2])
    es = [jnp.exp(l - mx) for l in lses]
    inv = 1.0 / (es[0] + es[1] + es[2])
    ex = ex_ref[...]
    out_a = None
    for e, o in zip(es, outs):
        wexp = _split_dot(e * inv, ex)
        term = wexp * o
        out_a = term if out_a is None else out_a + term
    mix_a = (out_a * ga_ref[...].astype(F32)).astype(BF16)
    f = _dot(mix_a, w_ref[:WIDTH_A, :]) + _dot(mb_ref[...], w_ref[WIDTH_A:, :])
    z = DEEPNORM_ALPHA * x_ref[...] + f
    mu = jnp.mean(z, axis=-1, keepdims=True)
    zc = z - mu
    var = jnp.mean(zc * zc, axis=-1, keepdims=True)
    y_ref[...] = zc * lax.rsqrt(var + LN_EPS) * lg_ref[...] + lb_ref[...]


def _outproj(x2, o_list, lse_list, ga, mix_b, w_out_bf16, expand, ln_g, ln_b):
    t = x2.shape[0]
    tm = TOKEN_TILE
    nt = o_list[0].shape[1] * o_list[0].shape[2] // tm
    row = lambda i: (i, 0)
    const = lambda i: (0, 0)
    wide = lambda w: pl.BlockSpec((tm, w), row)

    def by_residue(a):
        _, dil, _, width = a.shape
        return pl.BlockSpec((None, dil, tm // dil, width), lambda i: (i // nt, 0, i % nt, 0))

    n_strided = sum(o.shape[1] > 1 for o in o_list)
    return pl.pallas_call(
        _outproj_kernel,
        grid=(t // tm,),
        in_specs=[
            wide(D_MODEL),
            *[by_residue(o) for o in o_list],
            *[by_residue(l) for l in lse_list],
            wide(WIDTH_A), wide(WIDTH_B),
            pl.BlockSpec((WIDTH_A + WIDTH_B, D_MODEL), const),
            pl.BlockSpec((LANES, WIDTH_A), const),
            pl.BlockSpec((1, D_MODEL), const),
            pl.BlockSpec((1, D_MODEL), const),
        ],
        out_specs=wide(D_MODEL),
        out_shape=jax.ShapeDtypeStruct((t, D_MODEL), F32),
        scratch_shapes=[pltpu.VMEM((n_strided * WIDTH_A // LANES, tm, LANES), F32),
                        pltpu.VMEM((n_strided, tm, LANES), F32)],
        compiler_params=pltpu.CompilerParams(
            dimension_semantics=("arbitrary",), vmem_limit_bytes=VMEM_LIMIT),
        name="outproj",
    )(x2, *o_list, *lse_list, ga, mix_b, w_out_bf16, expand, ln_g, ln_b)


def _constants():
    head_of_lane = np.arange(LANES) // HEAD_DIM
    bd = (head_of_lane[:, None] == head_of_lane[None, :]).astype(np.float32)
    ex = (np.arange(LANES)[:, None] == (np.arange(WIDTH_A) // HEAD_DIM)[None, :]).astype(np.float32)
    return jnp.asarray(bd, BF16), jnp.asarray(ex, BF16)


def _layer(x, w_in_bf16, w_out_bf16, tables, qn, kn, ln_g, ln_b, bd, expand):
    b, n, _ = x.shape
    (qa, ka, va, qa4, ka4, va4, qa16, ka16, va16,
     ga, qb, gb, kb, vbt) = _inproj(x, w_in_bf16, tables, qn, kn, bd)
    mix_b = _gqa(qb.reshape(b, n, WIDTH_B), kb, vbt, gb.reshape(b, n, WIDTH_B))
    assert tuple(d for _, d in DILATED_PATTERNS) == (1, 4, 16)
    one = lambda a: a.reshape(b, 1, n, WIDTH_A)
    o_list, lse_list = [], []
    for qkv in ((one(qa), one(ka), one(va)), (qa4, ka4, va4), (qa16, ka16, va16)):
        o, lse = _dilated(*qkv)
        o_list.append(o)
        lse_list.append(lse)
    y = _outproj(x.reshape(b * n, D_MODEL), o_list, lse_list, ga, mix_b.reshape(b * n, WIDTH_B),
                 w_out_bf16, expand, ln_g, ln_b)
    return y.reshape(b, n, D_MODEL)


def kernel(x_prompt, x_sample, w_in, w_out, q_norm, k_norm, ln_g, ln_b):
    bd, expand = _constants()
    tables = _rope_tables(max(x_prompt.shape[1], x_sample.shape[1]))
    rep = LANES // HEAD_DIM
    y_prompt, y_sample = x_prompt, x_sample
    for i in range(w_in.shape[0]):
        w_in_bf16 = w_in[i].astype(BF16)
        w_out_bf16 = w_out[i].astype(BF16)
        qn = jnp.tile(q_norm[i].reshape(1, HEAD_DIM), (1, rep))
        kn = jnp.tile(k_norm[i].reshape(1, HEAD_DIM), (1, rep))
        args = (w_in_bf16, w_out_bf16, tables, qn, kn,
                ln_g[i].reshape(1, D_MODEL), ln_b[i].reshape(1, D_MODEL), bd, expand)
        y_prompt = _layer(y_prompt, *args)
        y_sample = _layer(y_sample, *args)
    return (y_prompt, y_sample)
```

```python
import functools
import math

import numpy as np
import jax
import jax.numpy as jnp
from jax import lax
from jax.experimental import pallas as pl
from jax.experimental.pallas import tpu as pltpu

D_MODEL = 1024
HEAD_DIM = 64
N_HEADS_A = 8
N_HEADS_B = 8
N_KV_B = 2
GROUP_B = N_HEADS_B // N_KV_B
WIDTH_A = N_HEADS_A * HEAD_DIM
WIDTH_B = N_HEADS_B * HEAD_DIM
KV_WIDTH_B = N_KV_B * HEAD_DIM
DILATED_PATTERNS = ((128, 1), (512, 4), (2048, 16))
GRID_W = 64
ROPE_THETA = 10000.0
LN_EPS = 1e-5
RMS_EPS = 1e-6
NEG_BIG = -1e30
DEPTH = 1
DEEPNORM_ALPHA = (2.0 * DEPTH) ** 0.25

OFF_QA = 0
OFF_KA = OFF_QA + WIDTH_A
OFF_VA = OFF_KA + WIDTH_A
OFF_GA = OFF_VA + WIDTH_A
OFF_QB = OFF_GA + WIDTH_A
OFF_KB = OFF_QB + WIDTH_B
OFF_VB = OFF_KB + KV_WIDTH_B
OFF_GB = OFF_VB + KV_WIDTH_B
IN_WIDTH = OFF_GB + WIDTH_B

LANES = 128
LOG2E = 1.4426950408889634
LN2 = 0.6931471805599453
SCORE_SCALE = (HEAD_DIM ** -0.5) * LOG2E

TOKEN_TILE = 512
GQA_TQ = 256
GQA_TK = 512
BF16_ROWS = 16
DIL_TQ = 128
DIL_HALF = 64
VMEM_LIMIT = 48 * 1024 * 1024

F32 = jnp.float32
BF16 = jnp.bfloat16


def _dot(a, b):
    return jnp.dot(a, b, preferred_element_type=F32)


def _dot_nt(a, b):
    return lax.dot_general(a, b, (((1,), (1,)), ((), ())), preferred_element_type=F32)


def _split_dot(x, w_bf16):
    hi = x.astype(BF16)
    lo = (x - hi.astype(F32)).astype(BF16)
    return _dot(hi, w_bf16) + _dot(lo, w_bf16)


def _rotate_half(x, first, half):
    return jnp.where(first, pltpu.roll(x, LANES - half, 1), pltpu.roll(x, half, 1))


def _inproj_kernel(x_ref, w_ref, ca_ref, sa_ref, cb_ref, sb_ref, qn_ref, kn_ref, bd_ref,
                   qa_ref, ka_ref, va_ref, qa4_ref, ka4_ref, va4_ref, qa16_ref, ka16_ref, va16_ref,
                   ga_ref, qb_ref, gb_ref, kbe_ref, kbo_ref, vbt_ref, slab_ref):
    tm = x_ref.shape[0]
    xb = x_ref[...].astype(BF16)

    def proj(off, width):
        return _dot(xb, w_ref[:, off:off + width])

    lane = lax.broadcasted_iota(jnp.int32, (tm, LANES), 1)
    first_a = (lane % HEAD_DIM) < (HEAD_DIM // 2)
    first_b = (lane % (HEAD_DIM // 2)) < (HEAD_DIM // 4)
    ca, sa = ca_ref[...], sa_ref[...]
    cb, sb = cb_ref[...], sb_ref[...]
    bd = bd_ref[...]

    def rope_a(x):
        return x * ca + _rotate_half(x, first_a, HEAD_DIM // 2) * sa

    def norm_rope_b(x, gain):
        ms = _split_dot(x * x, bd) * (1.0 / HEAD_DIM)
        xn = x * lax.rsqrt(ms + RMS_EPS) * gain
        return xn * cb + _rotate_half(xn, first_b, HEAD_DIM // 4) * sb

    def silu(g):
        return g * (1.0 / (1.0 + jnp.exp(-g)))

    n_slabs = WIDTH_A // LANES
    finish = (lambda x: rope_a(x) * SCORE_SCALE, rope_a, lambda x: x)
    for which, (off, nat_ref) in enumerate(((OFF_QA, qa_ref), (OFF_KA, ka_ref), (OFF_VA, va_ref))):
        h = proj(off, WIDTH_A)
        for c in range(n_slabs):
            cols = slice(c * LANES, (c + 1) * LANES)
            val = finish[which](h[:, cols])
            nat_ref[:, cols] = val.astype(BF16)
            slab_ref[which * n_slabs + c] = val
    for dil, refs in ((4, (qa4_ref, ka4_ref, va4_ref)), (16, (qa16_ref, ka16_ref, va16_ref))):
        rows = tm // dil
        for which in range(3):
            for c in range(n_slabs):
                cols = slice(c * LANES, (c + 1) * LANES)
                for r in range(dil):
                    part = slab_ref[which * n_slabs + c, pl.ds(r, rows, stride=dil), :]
                    refs[which][r, :, cols] = part.astype(BF16)
    ga_ref[...] = silu(proj(OFF_GA, WIDTH_A)).astype(BF16)

    h = proj(OFF_QB, WIDTH_B)
    qn = qn_ref[...]
    for c in range(WIDTH_B // LANES):
        qb = norm_rope_b(h[:, c * LANES:(c + 1) * LANES], qn) * SCORE_SCALE
        qb_ref[:, c * LANES:(c + 1) * LANES] = qb.astype(BF16)
    gb_ref[...] = silu(proj(OFF_GB, WIDTH_B)).astype(BF16)

    h = proj(OFF_KB, 2 * KV_WIDTH_B)
    kb = norm_rope_b(h[:, :KV_WIDTH_B], kn_ref[...])
    kb_swapped = pltpu.roll(kb, HEAD_DIM, 1)
    low = lane < HEAD_DIM
    zero = jnp.zeros_like(kb)
    kbe_ref[0] = jnp.where(low, kb, zero).astype(BF16)
    kbo_ref[0] = jnp.where(low, zero, kb_swapped).astype(BF16)
    kbe_ref[1] = jnp.where(low, kb_swapped, zero).astype(BF16)
    kbo_ref[1] = jnp.where(low, zero, kb).astype(BF16)
    vbt = h[:, KV_WIDTH_B:].T
    for j in range(N_KV_B):
        vbt_ref[j] = vbt[j * HEAD_DIM:(j + 1) * HEAD_DIM, :].astype(BF16)


def _rope_tables(n):
    def cos_sin(pos, dim):
        inv_freq = 1.0 / (ROPE_THETA ** (jnp.arange(0, dim, 2, dtype=F32) / dim))
        ang = pos.astype(F32)[:, None] * inv_freq[None, :]
        ang = jnp.concatenate([ang, ang], axis=-1)
        sign = jnp.concatenate([-jnp.ones((dim // 2,), F32), jnp.ones((dim // 2,), F32)])
        return jnp.cos(ang), jnp.sin(ang) * sign[None, :]

    pos = jnp.arange(n, dtype=jnp.int32)
    ca, sa = cos_sin(pos, HEAD_DIM)
    row = pos // GRID_W
    col = pos % GRID_W
    cr, sr = cos_sin(row, HEAD_DIM // 2)
    cc, sc = cos_sin(col, HEAD_DIM // 2)
    cb = jnp.concatenate([cr, cc], axis=-1)
    sb = jnp.concatenate([sr, sc], axis=-1)
    rep = LANES // HEAD_DIM
    return tuple(jnp.tile(t, (1, rep)) for t in (ca, sa, cb, sb))


def _inproj(x, w_bf16, tables, qn, kn, bd):
    b, n, _ = x.shape
    tm = TOKEN_TILE
    nt = n // tm
    t = b * n
    x2 = x.reshape(t, D_MODEL)
    row = lambda i: (i, 0)
    pos = lambda i: (i % nt, 0)
    const = lambda i: (0, 0)
    tab_spec = pl.BlockSpec((tm, LANES), pos)
    wide = lambda w: pl.BlockSpec((tm, w), row)
    def by_residue(dil):
        shape = jax.ShapeDtypeStruct((b, dil, n // dil, WIDTH_A), BF16)
        spec = pl.BlockSpec((None, dil, tm // dil, WIDTH_A), lambda i: (i // nt, 0, i % nt, 0))
        return [shape] * 3, [spec] * 3

    shapes4, specs4 = by_residue(4)
    shapes16, specs16 = by_residue(16)
    out_shapes = (
        jax.ShapeDtypeStruct((t, WIDTH_A), BF16),
        jax.ShapeDtypeStruct((t, WIDTH_A), BF16),
        jax.ShapeDtypeStruct((t, WIDTH_A), BF16),
        *shapes4, *shapes16,
        jax.ShapeDtypeStruct((t, WIDTH_A), BF16),
        jax.ShapeDtypeStruct((t, WIDTH_B), BF16),
        jax.ShapeDtypeStruct((t, WIDTH_B), BF16),
        jax.ShapeDtypeStruct((b, N_KV_B, n, LANES), BF16),
        jax.ShapeDtypeStruct((b, N_KV_B, n, LANES), BF16),
        jax.ShapeDtypeStruct((b, N_KV_B, HEAD_DIM, n), BF16),
    )
    out_specs = (
        wide(WIDTH_A), wide(WIDTH_A), wide(WIDTH_A), *specs4, *specs16,
        wide(WIDTH_A), wide(WIDTH_B), wide(WIDTH_B),
        pl.BlockSpec((None, N_KV_B, tm, LANES), lambda i: (i // nt, 0, i % nt, 0)),
        pl.BlockSpec((None, N_KV_B, tm, LANES), lambda i: (i // nt, 0, i % nt, 0)),
        pl.BlockSpec((None, N_KV_B, HEAD_DIM, tm), lambda i: (i // nt, 0, 0, i % nt)),
    )
    in_specs = [
        pl.BlockSpec((tm, D_MODEL), row),
        pl.BlockSpec((D_MODEL, IN_WIDTH), const),
        tab_spec, tab_spec, tab_spec, tab_spec,
        pl.BlockSpec((1, LANES), const),
        pl.BlockSpec((1, LANES), const),
        pl.BlockSpec((LANES, LANES), const),
    ]
    return pl.pallas_call(
        _inproj_kernel,
        grid=(t // tm,),
        in_specs=in_specs,
        out_specs=out_specs,
        out_shape=out_shapes,
        scratch_shapes=[pltpu.VMEM((3 * WIDTH_A // LANES, tm, LANES), F32)],
        compiler_params=pltpu.CompilerParams(
            dimension_semantics=("arbitrary",), vmem_limit_bytes=VMEM_LIMIT),
        name="inproj",
    )(x2, w_bf16, *tables, qn, kn, bd)


def _gqa_kernel(q_ref, ke_ref, ko_ref, vt_ref, g_ref, o_ref, s0_ref, s1_ref, m_ref, acc_ref):
    n = ke_ref.shape[0]
    tq, tk = GQA_TQ, GQA_TK
    nc = n // tk
    steps = (n // tq) * nc
    m_ref[...] = jnp.full(m_ref.shape, NEG_BIG, F32)
    acc_ref[...] = jnp.zeros(acc_ref.shape, F32)
    ones = jnp.ones((BF16_ROWS, tk), BF16)

    def offsets(t):
        qi, c = t // nc, t % nc
        if isinstance(t, int):
            return qi * tq, c * tk, c
        return pl.multiple_of(qi * tq, tq), pl.multiple_of(c * tk, tk), c

    def scores(t, s_ref):
        qoff, koff, _ = offsets(t)
        for pr in range(GROUP_B // 2):
            q_pair = q_ref[pl.ds(qoff, tq), pr * LANES:(pr + 1) * LANES]
            for odd, k_ref in enumerate((ke_ref, ko_ref)):
                h = 2 * pr + odd
                s_ref[:, h * tq:(h + 1) * tq] = _dot_nt(k_ref[pl.ds(koff, tk), :], q_pair)

    def update(t, s_ref):
        _, koff, c = offsets(t)
        s = s_ref[...]
        m_prev = jnp.where(c == 0, NEG_BIG, m_ref[...])
        m_new = jnp.maximum(m_prev, jnp.max(s, axis=0, keepdims=True))
        alpha = jnp.exp2(m_prev - m_new)
        p = jnp.exp2(s - m_new).astype(BF16)
        vt = jnp.concatenate([vt_ref[:, pl.ds(koff, tk)], ones], axis=0)
        acc_ref[...] = alpha * acc_ref[...] + _dot(vt, p)
        m_ref[...] = m_new

    def finalize(t):
        qoff, _, _ = offsets(t)
        acc = acc_ref[...]
        o_t = acc[:HEAD_DIM] * (1.0 / acc[HEAD_DIM:HEAD_DIM + 1])
        for pr in range(GROUP_B // 2):
            pair_t = jnp.concatenate([o_t[:, (2 * pr) * tq:(2 * pr + 1) * tq],
                                      o_t[:, (2 * pr + 1) * tq:(2 * pr + 2) * tq]], axis=0)
            gate = g_ref[pl.ds(qoff, tq), pr * LANES:(pr + 1) * LANES].astype(F32)
            o_ref[pl.ds(qoff, tq), pr * LANES:(pr + 1) * LANES] = (pair_t.T * gate).astype(BF16)

    scores(0, s0_ref)

    def body(tt, carry):
        t = 2 * tt
        scores(t + 1, s1_ref)
        update(t, s0_ref)
        scores(t + 2, s0_ref)
        update(t + 1, s1_ref)

        @pl.when((t + 1) % nc == nc - 1)
        def _():
            finalize(t + 1)

        return carry

    lax.fori_loop(0, steps // 2 - 1, body, 0)
    scores(steps - 1, s1_ref)
    update(steps - 2, s0_ref)
    update(steps - 1, s1_ref)
    finalize(steps - 1)


def _gqa(qb, kbe, kbo, vbt, gb):
    b, n, _ = qb.shape
    tq = GQA_TQ
    gw = GROUP_B * HEAD_DIM
    assert (n // GQA_TK) % 2 == 0 and n % tq == 0
    rows = pl.BlockSpec((None, n, gw), lambda bi, j: (bi, 0, j))
    keys = pl.BlockSpec((None, None, n, LANES), lambda bi, j: (bi, j, 0, 0))
    return pl.pallas_call(
        _gqa_kernel,
        grid=(b, N_KV_B),
        in_specs=[
            rows, keys, keys,
            pl.BlockSpec((None, None, HEAD_DIM, n), lambda bi, j: (bi, j, 0, 0)),
            rows,
        ],
        out_specs=rows,
        out_shape=jax.ShapeDtypeStruct((b, n, WIDTH_B), BF16),
        scratch_shapes=[
            pltpu.VMEM((GQA_TK, GROUP_B * tq), F32),
            pltpu.VMEM((GQA_TK, GROUP_B * tq), F32),
            pltpu.VMEM((1, GROUP_B * tq), F32),
            pltpu.VMEM((HEAD_DIM + BF16_ROWS, GROUP_B * tq), F32),
        ],
        compiler_params=pltpu.CompilerParams(
            dimension_semantics=("arbitrary", "arbitrary"),
            vmem_limit_bytes=VMEM_LIMIT),
        name="gqa",
    )(qb, kbe, kbo, vbt, gb)


def _dilated_kernel(q_ref, k_ref, v_ref, bias_ref, o_ref, lse_ref, s0_ref, s1_ref):
    dil, length = q_ref.shape[0], q_ref.shape[1]
    tq = DIL_TQ
    nb = length // tq
    steps = dil * nb
    win = min(tq + 2 * DIL_HALF, length)
    n_pairs = N_HEADS_A // 2
    low_lane = lax.broadcasted_iota(jnp.int32, (tq, LANES), 1) < HEAD_DIM
    low_row = lax.broadcasted_iota(jnp.int32, (LANES, tq), 0) < HEAD_DIM
    row8 = lax.broadcasted_iota(jnp.int32, (8, tq), 0)

    def window(t):
        r, i = t // nb, t % nb
        if isinstance(t, int):
            a0 = i * tq
            ws = min(max(a0 - DIL_HALF, 0), length - win)
            table = 0 if i == 0 else (2 if i == nb - 1 else 1)
        else:
            a0 = pl.multiple_of(i * tq, tq)
            ws = pl.multiple_of(jnp.clip(a0 - DIL_HALF, 0, length - win), DIL_HALF)
            table = jnp.where(i == 0, 0, jnp.where(i == nb - 1, 2, 1))
        return r, table, a0, ws

    def scores(t, s_ref):
        r, _, a0, ws = window(t)
        for pair in range(n_pairs):
            cols = slice(pair * LANES, (pair + 1) * LANES)
            q2 = q_ref[r, pl.ds(a0, tq), cols]
            k2 = k_ref[r, pl.ds(ws, win), cols]
            zero = jnp.zeros_like(q2)
            qs = jnp.concatenate([jnp.where(low_lane, q2, zero),
                                  jnp.where(low_lane, zero, q2)], axis=0)
            s_ref[pair] = _dot_nt(k2, qs)

    def attend(t, s_ref):
        r, table, a0, ws = window(t)
        bias = bias_ref[table]
        lse8 = jnp.zeros((8, tq), F32)
        for pair in range(n_pairs):
            cols = slice(pair * LANES, (pair + 1) * LANES)
            v2 = v_ref[r, pl.ds(ws, win), cols]
            s = s_ref[pair] + bias
            m = jnp.max(s, axis=0, keepdims=True)
            p = jnp.exp2(s - m)
            den = jnp.sum(p, axis=0, keepdims=True)
            o_t = lax.dot_general(v2, p.astype(BF16), (((0,), (0,)), ((), ())),
                                  preferred_element_type=F32)
            o_t = o_t * (1.0 / den)
            pair_t = jnp.where(low_row, o_t[:, :tq], o_t[:, tq:])
            o_ref[r, pl.ds(a0, tq), cols] = pair_t.T.astype(BF16)
            lse = (m + jnp.log2(den)) * LN2
            lse8 = jnp.where(row8 == 2 * pair, lse[:, :tq], lse8)
            lse8 = jnp.where(row8 == 2 * pair + 1, lse[:, tq:], lse8)
        lse_t = jnp.concatenate([lse8, jnp.zeros((LANES - 8, tq), F32)], axis=0)
        lse_ref[r, pl.ds(a0, tq), :] = lse_t.T

    scores(0, s0_ref)

    def body(tt, carry):
        t = 2 * tt
        scores(t + 1, s1_ref)
        attend(t, s0_ref)
        scores(t + 2, s0_ref)
        attend(t + 1, s1_ref)
        return carry

    lax.fori_loop(0, steps // 2 - 1, body, 0)
    scores(steps - 1, s1_ref)
    attend(steps - 2, s0_ref)
    attend(steps - 1, s1_ref)


def _dilated(qa, ka, va):
    b, dil, length, _ = qa.shape
    assert (dil * (length // DIL_TQ)) % 2 == 0
    win = min(DIL_TQ + 2 * DIL_HALF, length)
    s_scratch = pltpu.VMEM((N_HEADS_A // 2, win, 2 * DIL_TQ), F32)
    rel = np.arange(win)[:, None] - (np.arange(2 * DIL_TQ) % DIL_TQ)[None, :]
    bias = np.stack([np.where(np.abs(rel + off) <= DIL_HALF, 0.0, NEG_BIG)
                     for off in (0, -DIL_HALF, DIL_TQ - win)]).astype(np.float32)
    bias_spec = pl.BlockSpec(bias.shape, lambda bi: (0, 0, 0))
    spec = pl.BlockSpec((None, dil, length, WIDTH_A), lambda bi: (bi, 0, 0, 0))
    lse_spec = pl.BlockSpec((None, dil, length, LANES), lambda bi: (bi, 0, 0, 0))
    return pl.pallas_call(
        _dilated_kernel,
        grid=(b,),
        in_specs=[spec, spec, spec, bias_spec],
        out_specs=(spec, lse_spec),
        out_shape=(jax.ShapeDtypeStruct((b, dil, length, WIDTH_A), BF16),
                   jax.ShapeDtypeStruct((b, dil, length, LANES), F32)),
        scratch_shapes=[s_scratch, s_scratch],
        compiler_params=pltpu.CompilerParams(
            dimension_semantics=("arbitrary",), vmem_limit_bytes=VMEM_LIMIT),
        name=f"dilated{dil}",
    )(qa, ka, va, jnp.asarray(bias))


def _outproj_kernel(x_ref, o1_ref, o2_ref, o3_ref, l1_ref, l2_ref, l3_ref, ga_ref, mb_ref,
                    w_ref, ex_ref, lg_ref, lb_ref, y_ref, o_scr, l_scr):
    tm = x_ref.shape[0]
    n_slabs = WIDTH_A // LANES

    def token_order(o_ref, l_ref, slot):
        dil = o_ref.shape[0]
        if dil == 1:
            return o_ref[0].astype(F32), l_ref[0]
        rows = tm // dil
        for r in range(dil):
            for c in range(n_slabs):
                part = o_ref[r, :, c * LANES:(c + 1) * LANES].astype(F32)
                o_scr[slot * n_slabs + c, pl.ds(r, rows, stride=dil), :] = part
            l_scr[slot, pl.ds(r, rows, stride=dil), :] = l_ref[r]
        o = jnp.concatenate([o_scr[slot * n_slabs + c] for c in range(n_slabs)], axis=1)
        return o, l_scr[slot]

    outs, lses = zip(*(token_order(o_ref, l_ref, slot) for slot, (o_ref, l_ref) in
                       enumerate(((o2_ref, l2_ref), (o3_ref, l3_ref), (o1_ref, l1_ref)))))
    mx = jnp.maximum(jnp.maximum(lses[0], lses[1]), lses[2])
    es = [jnp.exp(l - mx) for l in lses]
    inv = 1.0 / (es[0] + es[1] + es[2])
    ex = ex_ref[...]
    out_a = None
    for e, o in zip(es, outs):
        wexp = _split_dot(e * inv, ex)
        term = wexp * o
        out_a = term if out_a is None else out_a + term
    mix_a = (out_a * ga_ref[...].astype(F32)).astype(BF16)
    f = _dot(mix_a, w_ref[:WIDTH_A, :]) + _dot(mb_ref[...], w_ref[WIDTH_A:, :])
    z = DEEPNORM_ALPHA * x_ref[...] + f
    mu = jnp.mean(z, axis=-1, keepdims=True)
    zc = z - mu
    var = jnp.mean(zc * zc, axis=-1, keepdims=True)
    y_ref[...] = zc * lax.rsqrt(var + LN_EPS) * lg_ref[...] + lb_ref[...]


def _outproj(x2, o_list, lse_list, ga, mix_b, w_out_bf16, expand, ln_g, ln_b):
    t = x2.shape[0]
    tm = TOKEN_TILE
    nt = o_list[0].shape[1] * o_list[0].shape[2] // tm
    row = lambda i: (i, 0)
    const = lambda i: (0, 0)
    wide = lambda w: pl.BlockSpec((tm, w), row)

    def by_residue(a):
        _, dil, _, width = a.shape
        return pl.BlockSpec((None, dil, tm // dil, width), lambda i: (i // nt, 0, i % nt, 0))

    n_strided = sum(o.shape[1] > 1 for o in o_list)
    return pl.pallas_call(
        _outproj_kernel,
        grid=(t // tm,),
        in_specs=[
            wide(D_MODEL),
            *[by_residue(o) for o in o_list],
            *[by_residue(l) for l in lse_list],
            wide(WIDTH_A), wide(WIDTH_B),
            pl.BlockSpec((WIDTH_A + WIDTH_B, D_MODEL), const),
            pl.BlockSpec((LANES, WIDTH_A), const),
            pl.BlockSpec((1, D_MODEL), const),
            pl.BlockSpec((1, D_MODEL), const),
        ],
        out_specs=wide(D_MODEL),
        out_shape=jax.ShapeDtypeStruct((t, D_MODEL), F32),
        scratch_shapes=[pltpu.VMEM((n_strided * WIDTH_A // LANES, tm, LANES), F32),
                        pltpu.VMEM((n_strided, tm, LANES), F32)],
        compiler_params=pltpu.CompilerParams(
            dimension_semantics=("arbitrary",), vmem_limit_bytes=VMEM_LIMIT),
        name="outproj",
    )(x2, *o_list, *lse_list, ga, mix_b, w_out_bf16, expand, ln_g, ln_b)


def _constants():
    head_of_lane = np.arange(LANES) // HEAD_DIM
    bd = (head_of_lane[:, None] == head_of_lane[None, :]).astype(np.float32)
    ex = (np.arange(LANES)[:, None] == (np.arange(WIDTH_A) // HEAD_DIM)[None, :]).astype(np.float32)
    return jnp.asarray(bd, BF16), jnp.asarray(ex, BF16)


def _layer(x, w_in_bf16, w_out_bf16, tables, qn, kn, ln_g, ln_b, bd, expand):
    b, n, _ = x.shape
    (qa, ka, va, qa4, ka4, va4, qa16, ka16, va16,
     ga, qb, gb, kbe, kbo, vbt) = _inproj(x, w_in_bf16, tables, qn, kn, bd)
    mix_b = _gqa(qb.reshape(b, n, WIDTH_B), kbe, kbo, vbt, gb.reshape(b, n, WIDTH_B))
    assert tuple(d for _, d in DILATED_PATTERNS) == (1, 4, 16)
    one = lambda a: a.reshape(b, 1, n, WIDTH_A)
    o_list, lse_list = [], []
    for qkv in ((one(qa), one(ka), one(va)), (qa4, ka4, va4), (qa16, ka16, va16)):
        o, lse = _dilated(*qkv)
        o_list.append(o)
        lse_list.append(lse)
    y = _outproj(x.reshape(b * n, D_MODEL), o_list, lse_list, ga, mix_b.reshape(b * n, WIDTH_B),
                 w_out_bf16, expand, ln_g, ln_b)
    return y.reshape(b, n, D_MODEL)


def kernel(x_prompt, x_sample, w_in, w_out, q_norm, k_norm, ln_g, ln_b):
    bd, expand = _constants()
    tables = _rope_tables(max(x_prompt.shape[1], x_sample.shape[1]))
    rep = LANES // HEAD_DIM
    y_prompt, y_sample = x_prompt, x_sample
    for i in range(w_in.shape[0]):
        w_in_bf16 = w_in[i].astype(BF16)
        w_out_bf16 = w_out[i].astype(BF16)
        qn = jnp.tile(q_norm[i].reshape(1, HEAD_DIM), (1, rep))
        kn = jnp.tile(k_norm[i].reshape(1, HEAD_DIM), (1, rep))
        args = (w_in_bf16, w_out_bf16, tables, qn, kn,
                ln_g[i].reshape(1, D_MODEL), ln_b[i].reshape(1, D_MODEL), bd, expand)
        y_prompt = _layer(y_prompt, *args)
        y_sample = _layer(y_sample, *args)
    return (y_prompt, y_sample)
```

```python
import functools
import math

import numpy as np
import jax
import jax.numpy as jnp
from jax import lax
from jax.experimental import pallas as pl
from jax.experimental.pallas import tpu as pltpu

D_MODEL = 1024
HEAD_DIM = 64
N_HEADS_A = 8
N_HEADS_B = 8
N_KV_B = 2
GROUP_B = N_HEADS_B // N_KV_B
WIDTH_A = N_HEADS_A * HEAD_DIM
WIDTH_B = N_HEADS_B * HEAD_DIM
KV_WIDTH_B = N_KV_B * HEAD_DIM
DILATED_PATTERNS = ((128, 1), (512, 4), (2048, 16))
GRID_W = 64
ROPE_THETA = 10000.0
LN_EPS = 1e-5
RMS_EPS = 1e-6
NEG_BIG = -1e30
DEPTH = 1
DEEPNORM_ALPHA = (2.0 * DEPTH) ** 0.25

OFF_QA = 0
OFF_KA = OFF_QA + WIDTH_A
OFF_VA = OFF_KA + WIDTH_A
OFF_GA = OFF_VA + WIDTH_A
OFF_QB = OFF_GA + WIDTH_A
OFF_KB = OFF_QB + WIDTH_B
OFF_VB = OFF_KB + KV_WIDTH_B
OFF_GB = OFF_VB + KV_WIDTH_B
IN_WIDTH = OFF_GB + WIDTH_B

LANES = 128
LOG2E = 1.4426950408889634
LN2 = 0.6931471805599453
SCORE_SCALE = (HEAD_DIM ** -0.5) * LOG2E

TOKEN_TILE = 512
GQA_TQ = 256
GQA_TK = 512
BF16_ROWS = 16
OUT_ROW_CHUNK = 128
DIL_TQ = 128
DIL_HALF = 64
VMEM_LIMIT = 48 * 1024 * 1024

F32 = jnp.float32
BF16 = jnp.bfloat16


def _dot(a, b):
    return jnp.dot(a, b, preferred_element_type=F32)


def _dot_nt(a, b):
    return lax.dot_general(a, b, (((1,), (1,)), ((), ())), preferred_element_type=F32)


def _split_dot(x, w_bf16):
    hi = x.astype(BF16)
    lo = (x - hi.astype(F32)).astype(BF16)
    return _dot(hi, w_bf16) + _dot(lo, w_bf16)


def _rotate_half(x, first, half):
    return jnp.where(first, pltpu.roll(x, LANES - half, 1), pltpu.roll(x, half, 1))


def _inproj_kernel(x_ref, w_ref, ca_ref, sa_ref, cb_ref, sb_ref, qn_ref, kn_ref, bd_ref,
                   qa_ref, ka_ref, va_ref, qa4_ref, ka4_ref, va4_ref, qa16_ref, ka16_ref, va16_ref,
                   ga_ref, qb_ref, gb_ref, kbe_ref, kbo_ref, vbt_ref, slab_ref, slab4_ref):
    tm = x_ref.shape[0]
    xb = x_ref[...].astype(BF16)

    def proj(off, width):
        return _dot(xb, w_ref[:, off:off + width])

    lane = lax.broadcasted_iota(jnp.int32, (tm, LANES), 1)
    first_a = (lane % HEAD_DIM) < (HEAD_DIM // 2)
    first_b = (lane % (HEAD_DIM // 2)) < (HEAD_DIM // 4)
    ca, sa = ca_ref[...], sa_ref[...]
    cb, sb = cb_ref[...], sb_ref[...]
    bd = bd_ref[...]

    def rope_a(x):
        return x * ca + _rotate_half(x, first_a, HEAD_DIM // 2) * sa

    def norm_rope_b(x, gain):
        ms = _split_dot(x * x, bd) * (1.0 / HEAD_DIM)
        xn = x * lax.rsqrt(ms + RMS_EPS) * gain
        return xn * cb + _rotate_half(xn, first_b, HEAD_DIM // 4) * sb

    def silu(g):
        return g * (1.0 / (1.0 + jnp.exp(-g)))

    n_slabs = WIDTH_A // LANES
    finish = (lambda x: rope_a(x) * SCORE_SCALE, rope_a, lambda x: x)
    for which, (off, nat_ref) in enumerate(((OFF_QA, qa_ref), (OFF_KA, ka_ref), (OFF_VA, va_ref))):
        h = proj(off, WIDTH_A)
        for c in range(n_slabs):
            cols = slice(c * LANES, (c + 1) * LANES)
            val = finish[which](h[:, cols])
            nat_ref[:, cols] = val.astype(BF16)
            slab_ref[which * n_slabs + c] = val
    for which, (ref4, ref16) in enumerate(((qa4_ref, qa16_ref), (ka4_ref, ka16_ref), (va4_ref, va16_ref))):
        for c in range(n_slabs):
            cols = slice(c * LANES, (c + 1) * LANES)
            base = (which * n_slabs + c) * 4
            for r4 in range(4):
                part = slab_ref[which * n_slabs + c, pl.ds(r4, tm // 4, stride=4), :]
                ref4[r4, :, cols] = part.astype(BF16)
                slab4_ref[base + r4] = part
            for r16 in range(16):
                part = slab4_ref[base + r16 % 4, pl.ds(r16 // 4, tm // 16, stride=4), :]
                ref16[r16, :, cols] = part.astype(BF16)
    ga_ref[...] = silu(proj(OFF_GA, WIDTH_A)).astype(BF16)

    h = proj(OFF_QB, WIDTH_B)
    qn = qn_ref[...]
    for c in range(WIDTH_B // LANES):
        qb = norm_rope_b(h[:, c * LANES:(c + 1) * LANES], qn) * SCORE_SCALE
        qb_ref[:, c * LANES:(c + 1) * LANES] = qb.astype(BF16)
    gb_ref[...] = silu(proj(OFF_GB, WIDTH_B)).astype(BF16)

    h = proj(OFF_KB, 2 * KV_WIDTH_B)
    kb = norm_rope_b(h[:, :KV_WIDTH_B], kn_ref[...])
    kb_swapped = pltpu.roll(kb, HEAD_DIM, 1)
    low = lane < HEAD_DIM
    zero = jnp.zeros_like(kb)
    kbe_ref[0] = jnp.where(low, kb, zero).astype(BF16)
    kbo_ref[0] = jnp.where(low, zero, kb_swapped).astype(BF16)
    kbe_ref[1] = jnp.where(low, kb_swapped, zero).astype(BF16)
    kbo_ref[1] = jnp.where(low, zero, kb).astype(BF16)
    vbt = h[:, KV_WIDTH_B:].T
    for j in range(N_KV_B):
        vbt_ref[j] = vbt[j * HEAD_DIM:(j + 1) * HEAD_DIM, :].astype(BF16)


def _rope_tables(n):
    def cos_sin(pos, dim):
        inv_freq = 1.0 / (ROPE_THETA ** (jnp.arange(0, dim, 2, dtype=F32) / dim))
        ang = pos.astype(F32)[:, None] * inv_freq[None, :]
        ang = jnp.concatenate([ang, ang], axis=-1)
        sign = jnp.concatenate([-jnp.ones((dim // 2,), F32), jnp.ones((dim // 2,), F32)])
        return jnp.cos(ang), jnp.sin(ang) * sign[None, :]

    pos = jnp.arange(n, dtype=jnp.int32)
    ca, sa = cos_sin(pos, HEAD_DIM)
    row = pos // GRID_W
    col = pos % GRID_W
    cr, sr = cos_sin(row, HEAD_DIM // 2)
    cc, sc = cos_sin(col, HEAD_DIM // 2)
    cb = jnp.concatenate([cr, cc], axis=-1)
    sb = jnp.concatenate([sr, sc], axis=-1)
    rep = LANES // HEAD_DIM
    return tuple(jnp.tile(t, (1, rep)) for t in (ca, sa, cb, sb))


def _inproj(x, w_bf16, tables, qn, kn, bd):
    b, n, _ = x.shape
    tm = TOKEN_TILE
    nt = n // tm
    t = b * n
    x2 = x.reshape(t, D_MODEL)
    row = lambda i: (i, 0)
    pos = lambda i: (i % nt, 0)
    const = lambda i: (0, 0)
    tab_spec = pl.BlockSpec((tm, LANES), pos)
    wide = lambda w: pl.BlockSpec((tm, w), row)
    def by_residue(dil):
        shape = jax.ShapeDtypeStruct((b, dil, n // dil, WIDTH_A), BF16)
        spec = pl.BlockSpec((None, dil, tm // dil, WIDTH_A), lambda i: (i // nt, 0, i % nt, 0))
        return [shape] * 3, [spec] * 3

    shapes4, specs4 = by_residue(4)
    shapes16, specs16 = by_residue(16)
    out_shapes = (
        jax.ShapeDtypeStruct((t, WIDTH_A), BF16),
        jax.ShapeDtypeStruct((t, WIDTH_A), BF16),
        jax.ShapeDtypeStruct((t, WIDTH_A), BF16),
        *shapes4, *shapes16,
        jax.ShapeDtypeStruct((t, WIDTH_A), BF16),
        jax.ShapeDtypeStruct((t, WIDTH_B), BF16),
        jax.ShapeDtypeStruct((t, WIDTH_B), BF16),
        jax.ShapeDtypeStruct((b, N_KV_B, n, LANES), BF16),
        jax.ShapeDtypeStruct((b, N_KV_B, n, LANES), BF16),
        jax.ShapeDtypeStruct((b, N_KV_B, HEAD_DIM, n), BF16),
    )
    out_specs = (
        wide(WIDTH_A), wide(WIDTH_A), wide(WIDTH_A), *specs4, *specs16,
        wide(WIDTH_A), wide(WIDTH_B), wide(WIDTH_B),
        pl.BlockSpec((None, N_KV_B, tm, LANES), lambda i: (i // nt, 0, i % nt, 0)),
        pl.BlockSpec((None, N_KV_B, tm, LANES), lambda i: (i // nt, 0, i % nt, 0)),
        pl.BlockSpec((None, N_KV_B, HEAD_DIM, tm), lambda i: (i // nt, 0, 0, i % nt)),
    )
    in_specs = [
        pl.BlockSpec((tm, D_MODEL), row),
        pl.BlockSpec((D_MODEL, IN_WIDTH), const),
        tab_spec, tab_spec, tab_spec, tab_spec,
        pl.BlockSpec((1, LANES), const),
        pl.BlockSpec((1, LANES), const),
        pl.BlockSpec((LANES, LANES), const),
    ]
    return pl.pallas_call(
        _inproj_kernel,
        grid=(t // tm,),
        in_specs=in_specs,
        out_specs=out_specs,
        out_shape=out_shapes,
        scratch_shapes=[pltpu.VMEM((3 * WIDTH_A // LANES, tm, LANES), F32),
                        pltpu.VMEM((4 * 3 * WIDTH_A // LANES, tm // 4, LANES), F32)],
        compiler_params=pltpu.CompilerParams(
            dimension_semantics=("arbitrary",), vmem_limit_bytes=VMEM_LIMIT),
        name="inproj",
    )(x2, w_bf16, *tables, qn, kn, bd)


def _gqa_kernel(q_ref, ke_ref, ko_ref, vt_ref, g_ref, o_ref, s0_ref, s1_ref, m_ref, acc_ref):
    n = ke_ref.shape[0]
    tq, tk = GQA_TQ, GQA_TK
    nc = n // tk
    steps = (n // tq) * nc
    m_ref[...] = jnp.full(m_ref.shape, NEG_BIG, F32)
    acc_ref[...] = jnp.zeros(acc_ref.shape, F32)
    ones = jnp.ones((BF16_ROWS, tk), BF16)

    def offsets(t):
        qi, c = t // nc, t % nc
        if isinstance(t, int):
            return qi * tq, c * tk, c
        return pl.multiple_of(qi * tq, tq), pl.multiple_of(c * tk, tk), c

    def scores(t, s_ref):
        qoff, koff, _ = offsets(t)
        for pr in range(GROUP_B // 2):
            q_pair = q_ref[pl.ds(qoff, tq), pr * LANES:(pr + 1) * LANES]
            for odd, k_ref in enumerate((ke_ref, ko_ref)):
                h = 2 * pr + odd
                s_ref[:, h * tq:(h + 1) * tq] = _dot_nt(k_ref[pl.ds(koff, tk), :], q_pair)

    def update(t, s_ref):
        _, koff, c = offsets(t)
        s = s_ref[...]
        m_prev = jnp.where(c == 0, NEG_BIG, m_ref[...])
        m_new = jnp.maximum(m_prev, jnp.max(s, axis=0, keepdims=True))
        alpha = jnp.exp2(m_prev - m_new)
        p = jnp.exp2(s - m_new).astype(BF16)
        vt = jnp.concatenate([vt_ref[:, pl.ds(koff, tk)], ones], axis=0)
        acc_ref[...] = alpha * acc_ref[...] + _dot(vt, p)
        m_ref[...] = m_new

    def finalize(t):
        qoff, _, _ = offsets(t)
        acc = acc_ref[...]
        o_t = acc[:HEAD_DIM] * (1.0 / acc[HEAD_DIM:HEAD_DIM + 1])
        for pr in range(GROUP_B // 2):
            pair_t = jnp.concatenate([o_t[:, (2 * pr) * tq:(2 * pr + 1) * tq],
                                      o_t[:, (2 * pr + 1) * tq:(2 * pr + 2) * tq]], axis=0)
            gate = g_ref[pl.ds(qoff, tq), pr * LANES:(pr + 1) * LANES].astype(F32)
            o_ref[pl.ds(qoff, tq), pr * LANES:(pr + 1) * LANES] = (pair_t.T * gate).astype(BF16)

    scores(0, s0_ref)

    def body(tt, carry):
        t = 2 * tt
        scores(t + 1, s1_ref)
        update(t, s0_ref)
        scores(t + 2, s0_ref)
        update(t + 1, s1_ref)

        @pl.when((t + 1) % nc == nc - 1)
        def _():
            finalize(t + 1)

        return carry

    lax.fori_loop(0, steps // 2 - 1, body, 0)
    scores(steps - 1, s1_ref)
    update(steps - 2, s0_ref)
    update(steps - 1, s1_ref)
    finalize(steps - 1)


def _gqa(qb, kbe, kbo, vbt, gb):
    b, n, _ = qb.shape
    tq = GQA_TQ
    gw = GROUP_B * HEAD_DIM
    assert (n // GQA_TK) % 2 == 0 and n % tq == 0
    rows = pl.BlockSpec((None, n, gw), lambda bi, j: (bi, 0, j))
    keys = pl.BlockSpec((None, None, n, LANES), lambda bi, j: (bi, j, 0, 0))
    return pl.pallas_call(
        _gqa_kernel,
        grid=(b, N_KV_B),
        in_specs=[
            rows, keys, keys,
            pl.BlockSpec((None, None, HEAD_DIM, n), lambda bi, j: (bi, j, 0, 0)),
            rows,
        ],
        out_specs=rows,
        out_shape=jax.ShapeDtypeStruct((b, n, WIDTH_B), BF16),
        scratch_shapes=[
            pltpu.VMEM((GQA_TK, GROUP_B * tq), F32),
            pltpu.VMEM((GQA_TK, GROUP_B * tq), F32),
            pltpu.VMEM((1, GROUP_B * tq), F32),
            pltpu.VMEM((HEAD_DIM + BF16_ROWS, GROUP_B * tq), F32),
        ],
        compiler_params=pltpu.CompilerParams(
            dimension_semantics=("arbitrary", "arbitrary"),
            vmem_limit_bytes=VMEM_LIMIT),
        name="gqa",
    )(qb, kbe, kbo, vbt, gb)


def _dilated_kernel(q_ref, k_ref, v_ref, bias_ref, o_ref, lse_ref, s0_ref, s1_ref):
    dil, length = q_ref.shape[0], q_ref.shape[1]
    tq = DIL_TQ
    nb = length // tq
    steps = dil * nb
    win = min(tq + 2 * DIL_HALF, length)
    n_pairs = N_HEADS_A // 2
    low_lane = lax.broadcasted_iota(jnp.int32, (tq, LANES), 1) < HEAD_DIM
    low_row = lax.broadcasted_iota(jnp.int32, (LANES, tq), 0) < HEAD_DIM
    row8 = lax.broadcasted_iota(jnp.int32, (8, tq), 0)

    def window(t):
        r, i = t // nb, t % nb
        if isinstance(t, int):
            a0 = i * tq
            ws = min(max(a0 - DIL_HALF, 0), length - win)
            table = 0 if i == 0 else (2 if i == nb - 1 else 1)
        else:
            a0 = pl.multiple_of(i * tq, tq)
            ws = pl.multiple_of(jnp.clip(a0 - DIL_HALF, 0, length - win), DIL_HALF)
            table = jnp.where(i == 0, 0, jnp.where(i == nb - 1, 2, 1))
        return r, table, a0, ws

    def scores(t, s_ref):
        r, _, a0, ws = window(t)
        for pair in range(n_pairs):
            cols = slice(pair * LANES, (pair + 1) * LANES)
            q2 = q_ref[r, pl.ds(a0, tq), cols]
            k2 = k_ref[r, pl.ds(ws, win), cols]
            zero = jnp.zeros_like(q2)
            qs = jnp.concatenate([jnp.where(low_lane, q2, zero),
                                  jnp.where(low_lane, zero, q2)], axis=0)
            s_ref[pair] = _dot_nt(k2, qs)

    def attend(t, s_ref):
        r, table, a0, ws = window(t)
        bias = bias_ref[table]
        lse8 = jnp.zeros((8, tq), F32)
        for pair in range(n_pairs):
            cols = slice(pair * LANES, (pair + 1) * LANES)
            v2 = v_ref[r, pl.ds(ws, win), cols]
            s = s_ref[pair] + bias
            m = jnp.max(s, axis=0, keepdims=True)
            p = jnp.exp2(s - m)
            den = jnp.sum(p, axis=0, keepdims=True)
            o_t = lax.dot_general(v2, p.astype(BF16), (((0,), (0,)), ((), ())),
                                  preferred_element_type=F32)
            o_t = o_t * (1.0 / den)
            pair_t = jnp.where(low_row, o_t[:, :tq], o_t[:, tq:])
            o_ref[r, pl.ds(a0, tq), cols] = pair_t.T.astype(BF16)
            lse = (m + jnp.log2(den)) * LN2
            lse8 = jnp.where(row8 == 2 * pair, lse[:, :tq], lse8)
            lse8 = jnp.where(row8 == 2 * pair + 1, lse[:, tq:], lse8)
        lse_t = jnp.concatenate([lse8, jnp.zeros((LANES - 8, tq), F32)], axis=0)
        lse_ref[r, pl.ds(a0, tq), :] = lse_t.T

    scores(0, s0_ref)

    def body(tt, carry):
        t = 2 * tt
        scores(t + 1, s1_ref)
        attend(t, s0_ref)
        scores(t + 2, s0_ref)
        attend(t + 1, s1_ref)
        return carry

    lax.fori_loop(0, steps // 2 - 1, body, 0)
    scores(steps - 1, s1_ref)
    attend(steps - 2, s0_ref)
    attend(steps - 1, s1_ref)


def _dilated(qa, ka, va):
    b, dil, length, _ = qa.shape
    assert (dil * (length // DIL_TQ)) % 2 == 0
    win = min(DIL_TQ + 2 * DIL_HALF, length)
    s_scratch = pltpu.VMEM((N_HEADS_A // 2, win, 2 * DIL_TQ), F32)
    rel = np.arange(win)[:, None] - (np.arange(2 * DIL_TQ) % DIL_TQ)[None, :]
    bias = np.stack([np.where(np.abs(rel + off) <= DIL_HALF, 0.0, NEG_BIG)
                     for off in (0, -DIL_HALF, DIL_TQ - win)]).astype(np.float32)
    bias_spec = pl.BlockSpec(bias.shape, lambda bi: (0, 0, 0))
    spec = pl.BlockSpec((None, dil, length, WIDTH_A), lambda bi: (bi, 0, 0, 0))
    lse_spec = pl.BlockSpec((None, dil, length, LANES), lambda bi: (bi, 0, 0, 0))
    return pl.pallas_call(
        _dilated_kernel,
        grid=(b,),
        in_specs=[spec, spec, spec, bias_spec],
        out_specs=(spec, lse_spec),
        out_shape=(jax.ShapeDtypeStruct((b, dil, length, WIDTH_A), BF16),
                   jax.ShapeDtypeStruct((b, dil, length, LANES), F32)),
        scratch_shapes=[s_scratch, s_scratch],
        compiler_params=pltpu.CompilerParams(
            dimension_semantics=("arbitrary",), vmem_limit_bytes=VMEM_LIMIT),
        name=f"dilated{dil}",
    )(qa, ka, va, jnp.asarray(bias))


def _outproj_kernel(x_ref, o1_ref, o2_ref, o3_ref, l1_ref, l2_ref, l3_ref, ga_ref, mb_ref,
                    w_ref, ex_ref, lg_ref, lb_ref, y_ref, o_scr, l_scr, mix_scr):
    tm = x_ref.shape[0]
    n_slabs = WIDTH_A // LANES
    rc = OUT_ROW_CHUNK

    for slot, (o_ref, l_ref) in enumerate(((o2_ref, l2_ref), (o3_ref, l3_ref))):
        dil = o_ref.shape[0]
        for r in range(dil):
            for c in range(n_slabs):
                part = o_ref[r, :, c * LANES:(c + 1) * LANES].astype(F32)
                o_scr[slot * n_slabs + c, pl.ds(r, tm // dil, stride=dil), :] = part
            l_scr[slot, pl.ds(r, tm // dil, stride=dil), :] = l_ref[r]

    ex = ex_ref[...]
    for ch in range(tm // rc):
        rows = slice(ch * rc, (ch + 1) * rc)
        lses = (l1_ref[0, rows, :], l_scr[0, rows, :], l_scr[1, rows, :])
        mx = jnp.maximum(jnp.maximum(lses[0], lses[1]), lses[2])
        es = [jnp.exp(l - mx) for l in lses]
        inv = 1.0 / (es[0] + es[1] + es[2])
        o1 = o1_ref[0, rows, :].astype(F32)
        out_a = o1
        for slot in range(2):
            w = _dot((es[slot + 1] * inv).astype(BF16), ex)
            o = jnp.concatenate([o_scr[slot * n_slabs + c, rows, :] for c in range(n_slabs)], axis=1)
            out_a = out_a + w * (o - o1)
        mix_scr[rows, :] = (out_a * ga_ref[rows, :].astype(F32)).astype(BF16)

    f = _dot(mix_scr[...], w_ref[:WIDTH_A, :]) + _dot(mb_ref[...], w_ref[WIDTH_A:, :])
    for ch in range(tm // rc):
        rows = slice(ch * rc, (ch + 1) * rc)
        z = DEEPNORM_ALPHA * x_ref[rows, :] + f[rows, :]
        mu = jnp.mean(z, axis=-1, keepdims=True)
        zc = z - mu
        var = jnp.mean(zc * zc, axis=-1, keepdims=True)
        y_ref[rows, :] = zc * lax.rsqrt(var + LN_EPS) * lg_ref[...] + lb_ref[...]


def _outproj(x2, o_list, lse_list, ga, mix_b, w_out_bf16, expand, ln_g, ln_b):
    t = x2.shape[0]
    tm = TOKEN_TILE
    nt = o_list[0].shape[1] * o_list[0].shape[2] // tm
    row = lambda i: (i, 0)
    const = lambda i: (0, 0)
    wide = lambda w: pl.BlockSpec((tm, w), row)

    def by_residue(a):
        _, dil, _, width = a.shape
        return pl.BlockSpec((None, dil, tm // dil, width), lambda i: (i // nt, 0, i % nt, 0))

    n_strided = sum(o.shape[1] > 1 for o in o_list)
    return pl.pallas_call(
        _outproj_kernel,
        grid=(t // tm,),
        in_specs=[
            wide(D_MODEL),
            *[by_residue(o) for o in o_list],
            *[by_residue(l) for l in lse_list],
            wide(WIDTH_A), wide(WIDTH_B),
            pl.BlockSpec((WIDTH_A + WIDTH_B, D_MODEL), const),
            pl.BlockSpec((LANES, WIDTH_A), const),
            pl.BlockSpec((1, D_MODEL), const),
            pl.BlockSpec((1, D_MODEL), const),
        ],
        out_specs=wide(D_MODEL),
        out_shape=jax.ShapeDtypeStruct((t, D_MODEL), F32),
        scratch_shapes=[pltpu.VMEM((n_strided * WIDTH_A // LANES, tm, LANES), F32),
                        pltpu.VMEM((n_strided, tm, LANES), F32),
                        pltpu.VMEM((tm, WIDTH_A), BF16)],
        compiler_params=pltpu.CompilerParams(
            dimension_semantics=("arbitrary",), vmem_limit_bytes=VMEM_LIMIT),
        name="outproj",
    )(x2, *o_list, *lse_list, ga, mix_b, w_out_bf16, expand, ln_g, ln_b)


def _constants():
    head_of_lane = np.arange(LANES) // HEAD_DIM
    bd = (head_of_lane[:, None] == head_of_lane[None, :]).astype(np.float32)
    ex = (np.arange(LANES)[:, None] == (np.arange(WIDTH_A) // HEAD_DIM)[None, :]).astype(np.float32)
    return jnp.asarray(bd, BF16), jnp.asarray(ex, BF16)


def _layer(x, w_in_bf16, w_out_bf16, tables, qn, kn, ln_g, ln_b, bd, expand):
    b, n, _ = x.shape
    (qa, ka, va, qa4, ka4, va4, qa16, ka16, va16,
     ga, qb, gb, kbe, kbo, vbt) = _inproj(x, w_in_bf16, tables, qn, kn, bd)
    mix_b = _gqa(qb.reshape(b, n, WIDTH_B), kbe, kbo, vbt, gb.reshape(b, n, WIDTH_B))
    assert tuple(d for _, d in DILATED_PATTERNS) == (1, 4, 16)
    one = lambda a: a.reshape(b, 1, n, WIDTH_A)
    o_list, lse_list = [], []
    for qkv in ((one(qa), one(ka), one(va)), (qa4, ka4, va4), (qa16, ka16, va16)):
        o, lse = _dilated(*qkv)
        o_list.append(o)
        lse_list.append(lse)
    y = _outproj(x.reshape(b * n, D_MODEL), o_list, lse_list, ga, mix_b.reshape(b * n, WIDTH_B),
                 w_out_bf16, expand, ln_g, ln_b)
    return y.reshape(b, n, D_MODEL)


def kernel(x_prompt, x_sample, w_in, w_out, q_norm, k_norm, ln_g, ln_b):
    bd, expand = _constants()
    tables = _rope_tables(max(x_prompt.shape[1], x_sample.shape[1]))
    rep = LANES // HEAD_DIM
    y_prompt, y_sample = x_prompt, x_sample
    for i in range(w_in.shape[0]):
        w_in_bf16 = w_in[i].astype(BF16)
        w_out_bf16 = w_out[i].astype(BF16)
        qn = jnp.tile(q_norm[i].reshape(1, HEAD_DIM), (1, rep))
        kn = jnp.tile(k_norm[i].reshape(1, HEAD_DIM), (1, rep))
        args = (w_in_bf16, w_out_bf16, tables, qn, kn,
                ln_g[i].reshape(1, D_MODEL), ln_b[i].reshape(1, D_MODEL), bd, expand)
        y_prompt = _layer(y_prompt, *args)
        y_sample = _layer(y_sample, *args)
    return (y_prompt, y_sample)
```

```python
import functools
import math

import numpy as np
import jax
import jax.numpy as jnp
from jax import lax
from jax.experimental import pallas as pl
from jax.experimental.pallas import tpu as pltpu

D_MODEL = 1024
HEAD_DIM = 64
N_HEADS_A = 8
N_HEADS_B = 8
N_KV_B = 2
GROUP_B = N_HEADS_B // N_KV_B
WIDTH_A = N_HEADS_A * HEAD_DIM
WIDTH_B = N_HEADS_B * HEAD_DIM
KV_WIDTH_B = N_KV_B * HEAD_DIM
DILATED_PATTERNS = ((128, 1), (512, 4), (2048, 16))
GRID_W = 64
ROPE_THETA = 10000.0
LN_EPS = 1e-5
RMS_EPS = 1e-6
NEG_BIG = -1e30
DEPTH = 1
DEEPNORM_ALPHA = (2.0 * DEPTH) ** 0.25

OFF_QA = 0
OFF_KA = OFF_QA + WIDTH_A
OFF_VA = OFF_KA + WIDTH_A
OFF_GA = OFF_VA + WIDTH_A
OFF_QB = OFF_GA + WIDTH_A
OFF_KB = OFF_QB + WIDTH_B
OFF_VB = OFF_KB + KV_WIDTH_B
OFF_GB = OFF_VB + KV_WIDTH_B
IN_WIDTH = OFF_GB + WIDTH_B

LANES = 128
LOG2E = 1.4426950408889634
LN2 = 0.6931471805599453
SCORE_SCALE = (HEAD_DIM ** -0.5) * LOG2E

TOKEN_TILE = 512
GQA_TQ = 256
GQA_TK = 512
GQA_UNROLL = 4
BF16_ROWS = 16
OUT_ROW_CHUNK = 128
DIL_TQ = 128
DIL_HALF = 64
VMEM_LIMIT = 48 * 1024 * 1024

F32 = jnp.float32
BF16 = jnp.bfloat16


def _dot(a, b):
    return jnp.dot(a, b, preferred_element_type=F32)


def _dot_nt(a, b):
    return lax.dot_general(a, b, (((1,), (1,)), ((), ())), preferred_element_type=F32)


def _split_dot(x, w_bf16):
    hi = x.astype(BF16)
    lo = (x - hi.astype(F32)).astype(BF16)
    return _dot(hi, w_bf16) + _dot(lo, w_bf16)


def _rotate_half(x, first, half):
    return jnp.where(first, pltpu.roll(x, LANES - half, 1), pltpu.roll(x, half, 1))


def _inproj_kernel(x_ref, w_ref, ca_ref, sa_ref, cb_ref, sb_ref, qn_ref, kn_ref, bd_ref,
                   qa_ref, ka_ref, va_ref, qa4_ref, ka4_ref, va4_ref, qa16_ref, ka16_ref, va16_ref,
                   ga_ref, qb_ref, gb_ref, kbe_ref, kbo_ref, vbt_ref, slab_ref, slab4_ref):
    tm = x_ref.shape[0]
    xb = x_ref[...].astype(BF16)

    def proj(off, width):
        return _dot(xb, w_ref[:, off:off + width])

    lane = lax.broadcasted_iota(jnp.int32, (tm, LANES), 1)
    first_a = (lane % HEAD_DIM) < (HEAD_DIM // 2)
    first_b = (lane % (HEAD_DIM // 2)) < (HEAD_DIM // 4)
    ca, sa = ca_ref[...], sa_ref[...]
    cb, sb = cb_ref[...], sb_ref[...]
    bd = bd_ref[...]

    def rope_a(x):
        return x * ca + _rotate_half(x, first_a, HEAD_DIM // 2) * sa

    def norm_rope_b(x, gain):
        ms = _split_dot(x * x, bd) * (1.0 / HEAD_DIM)
        xn = x * lax.rsqrt(ms + RMS_EPS) * gain
        return xn * cb + _rotate_half(xn, first_b, HEAD_DIM // 4) * sb

    def silu(g):
        return g * (1.0 / (1.0 + jnp.exp(-g)))

    n_slabs = WIDTH_A // LANES
    finish = (lambda x: rope_a(x) * SCORE_SCALE, rope_a, lambda x: x)
    for which, (off, nat_ref) in enumerate(((OFF_QA, qa_ref), (OFF_KA, ka_ref), (OFF_VA, va_ref))):
        h = proj(off, WIDTH_A)
        for c in range(n_slabs):
            cols = slice(c * LANES, (c + 1) * LANES)
            val = finish[which](h[:, cols])
            nat_ref[:, cols] = val.astype(BF16)
            slab_ref[which * n_slabs + c] = val
    for which, (ref4, ref16) in enumerate(((qa4_ref, qa16_ref), (ka4_ref, ka16_ref), (va4_ref, va16_ref))):
        for c in range(n_slabs):
            cols = slice(c * LANES, (c + 1) * LANES)
            base = (which * n_slabs + c) * 4
            for r4 in range(4):
                part = slab_ref[which * n_slabs + c, pl.ds(r4, tm // 4, stride=4), :]
                ref4[r4, :, cols] = part.astype(BF16)
                slab4_ref[base + r4] = part
            for r16 in range(16):
                part = slab4_ref[base + r16 % 4, pl.ds(r16 // 4, tm // 16, stride=4), :]
                ref16[r16, :, cols] = part.astype(BF16)
    ga_ref[...] = silu(proj(OFF_GA, WIDTH_A)).astype(BF16)

    h = proj(OFF_QB, WIDTH_B)
    qn = qn_ref[...]
    for c in range(WIDTH_B // LANES):
        qb = norm_rope_b(h[:, c * LANES:(c + 1) * LANES], qn) * SCORE_SCALE
        qb_ref[:, c * LANES:(c + 1) * LANES] = qb.astype(BF16)
    gb_ref[...] = silu(proj(OFF_GB, WIDTH_B)).astype(BF16)

    h = proj(OFF_KB, 2 * KV_WIDTH_B)
    kb = norm_rope_b(h[:, :KV_WIDTH_B], kn_ref[...])
    kb_swapped = pltpu.roll(kb, HEAD_DIM, 1)
    low = lane < HEAD_DIM
    zero = jnp.zeros_like(kb)
    kbe_ref[0] = jnp.where(low, kb, zero).astype(BF16)
    kbo_ref[0] = jnp.where(low, zero, kb_swapped).astype(BF16)
    kbe_ref[1] = jnp.where(low, kb_swapped, zero).astype(BF16)
    kbo_ref[1] = jnp.where(low, zero, kb).astype(BF16)
    vbt = h[:, KV_WIDTH_B:].T
    for j in range(N_KV_B):
        vbt_ref[j] = vbt[j * HEAD_DIM:(j + 1) * HEAD_DIM, :].astype(BF16)


def _rope_tables(n):
    def cos_sin(pos, dim):
        inv_freq = 1.0 / (ROPE_THETA ** (jnp.arange(0, dim, 2, dtype=F32) / dim))
        ang = pos.astype(F32)[:, None] * inv_freq[None, :]
        ang = jnp.concatenate([ang, ang], axis=-1)
        sign = jnp.concatenate([-jnp.ones((dim // 2,), F32), jnp.ones((dim // 2,), F32)])
        return jnp.cos(ang), jnp.sin(ang) * sign[None, :]

    pos = jnp.arange(n, dtype=jnp.int32)
    ca, sa = cos_sin(pos, HEAD_DIM)
    row = pos // GRID_W
    col = pos % GRID_W
    cr, sr = cos_sin(row, HEAD_DIM // 2)
    cc, sc = cos_sin(col, HEAD_DIM // 2)
    cb = jnp.concatenate([cr, cc], axis=-1)
    sb = jnp.concatenate([sr, sc], axis=-1)
    rep = LANES // HEAD_DIM
    return tuple(jnp.tile(t, (1, rep)) for t in (ca, sa, cb, sb))


def _inproj(x, w_bf16, tables, qn, kn, bd):
    b, n, _ = x.shape
    tm = TOKEN_TILE
    nt = n // tm
    t = b * n
    x2 = x.reshape(t, D_MODEL)
    row = lambda i: (i, 0)
    pos = lambda i: (i % nt, 0)
    const = lambda i: (0, 0)
    tab_spec = pl.BlockSpec((tm, LANES), pos)
    wide = lambda w: pl.BlockSpec((tm, w), row)
    def by_residue(dil):
        shape = jax.ShapeDtypeStruct((b, dil, n // dil, WIDTH_A), BF16)
        spec = pl.BlockSpec((None, dil, tm // dil, WIDTH_A), lambda i: (i // nt, 0, i % nt, 0))
        return [shape] * 3, [spec] * 3

    shapes4, specs4 = by_residue(4)
    shapes16, specs16 = by_residue(16)
    out_shapes = (
        jax.ShapeDtypeStruct((t, WIDTH_A), BF16),
        jax.ShapeDtypeStruct((t, WIDTH_A), BF16),
        jax.ShapeDtypeStruct((t, WIDTH_A), BF16),
        *shapes4, *shapes16,
        jax.ShapeDtypeStruct((t, WIDTH_A), BF16),
        jax.ShapeDtypeStruct((t, WIDTH_B), BF16),
        jax.ShapeDtypeStruct((t, WIDTH_B), BF16),
        jax.ShapeDtypeStruct((b, N_KV_B, n, LANES), BF16),
        jax.ShapeDtypeStruct((b, N_KV_B, n, LANES), BF16),
        jax.ShapeDtypeStruct((b, N_KV_B, HEAD_DIM, n), BF16),
    )
    out_specs = (
        wide(WIDTH_A), wide(WIDTH_A), wide(WIDTH_A), *specs4, *specs16,
        wide(WIDTH_A), wide(WIDTH_B), wide(WIDTH_B),
        pl.BlockSpec((None, N_KV_B, tm, LANES), lambda i: (i // nt, 0, i % nt, 0)),
        pl.BlockSpec((None, N_KV_B, tm, LANES), lambda i: (i // nt, 0, i % nt, 0)),
        pl.BlockSpec((None, N_KV_B, HEAD_DIM, tm), lambda i: (i // nt, 0, 0, i % nt)),
    )
    in_specs = [
        pl.BlockSpec((tm, D_MODEL), row),
        pl.BlockSpec((D_MODEL, IN_WIDTH), const),
        tab_spec, tab_spec, tab_spec, tab_spec,
        pl.BlockSpec((1, LANES), const),
        pl.BlockSpec((1, LANES), const),
        pl.BlockSpec((LANES, LANES), const),
    ]
    return pl.pallas_call(
        _inproj_kernel,
        grid=(t // tm,),
        in_specs=in_specs,
        out_specs=out_specs,
        out_shape=out_shapes,
        scratch_shapes=[pltpu.VMEM((3 * WIDTH_A // LANES, tm, LANES), F32),
                        pltpu.VMEM((4 * 3 * WIDTH_A // LANES, tm // 4, LANES), F32)],
        compiler_params=pltpu.CompilerParams(
            dimension_semantics=("arbitrary",), vmem_limit_bytes=VMEM_LIMIT),
        name="inproj",
    )(x2, w_bf16, *tables, qn, kn, bd)


def _gqa_kernel(q_ref, ke_ref, ko_ref, vt_ref, g_ref, o_ref, s0_ref, s1_ref, m_ref, acc_ref):
    n = ke_ref.shape[0]
    tq, tk = GQA_TQ, GQA_TK
    nc = n // tk
    steps = (n // tq) * nc
    m_ref[...] = jnp.full(m_ref.shape, NEG_BIG, F32)
    acc_ref[...] = jnp.zeros(acc_ref.shape, F32)
    ones = jnp.ones((BF16_ROWS, tk), BF16)

    def offsets(t):
        qi, c = t // nc, t % nc
        if isinstance(t, int):
            return qi * tq, c * tk, c
        return pl.multiple_of(qi * tq, tq), pl.multiple_of(c * tk, tk), c

    def scores(t, s_ref):
        qoff, koff, _ = offsets(t)
        for pr in range(GROUP_B // 2):
            q_pair = q_ref[pl.ds(qoff, tq), pr * LANES:(pr + 1) * LANES]
            for odd, k_ref in enumerate((ke_ref, ko_ref)):
                h = 2 * pr + odd
                s_ref[:, h * tq:(h + 1) * tq] = _dot_nt(k_ref[pl.ds(koff, tk), :], q_pair)

    def update(t, s_ref):
        _, koff, c = offsets(t)
        s = s_ref[...]
        m_prev = jnp.where(c == 0, NEG_BIG, m_ref[...])
        m_new = jnp.maximum(m_prev, jnp.max(s, axis=0, keepdims=True))
        alpha = jnp.exp2(m_prev - m_new)
        p = jnp.exp2(s - m_new).astype(BF16)
        vt = jnp.concatenate([vt_ref[:, pl.ds(koff, tk)], ones], axis=0)
        acc_ref[...] = alpha * acc_ref[...] + _dot(vt, p)
        m_ref[...] = m_new

    def finalize(t):
        qoff, _, _ = offsets(t)
        acc = acc_ref[...]
        o_t = acc[:HEAD_DIM] * (1.0 / acc[HEAD_DIM:HEAD_DIM + 1])
        for pr in range(GROUP_B // 2):
            pair_t = jnp.concatenate([o_t[:, (2 * pr) * tq:(2 * pr + 1) * tq],
                                      o_t[:, (2 * pr + 1) * tq:(2 * pr + 2) * tq]], axis=0)
            gate = g_ref[pl.ds(qoff, tq), pr * LANES:(pr + 1) * LANES].astype(F32)
            o_ref[pl.ds(qoff, tq), pr * LANES:(pr + 1) * LANES] = (pair_t.T * gate).astype(BF16)

    unroll = GQA_UNROLL
    bufs = (s0_ref, s1_ref)
    scores(0, bufs[0])

    def body(tt, carry):
        t = unroll * tt
        for u in range(unroll):
            scores(t + u + 1, bufs[(u + 1) % 2])
            update(t + u, bufs[u % 2])

        @pl.when((t + unroll - 1) % nc == nc - 1)
        def _():
            finalize(t + unroll - 1)

        return carry

    lax.fori_loop(0, steps // unroll - 1, body, 0)
    for u in range(unroll):
        t = steps - unroll + u
        if u < unroll - 1:
            scores(t + 1, bufs[(u + 1) % 2])
        update(t, bufs[u % 2])
    finalize(steps - 1)


def _gqa(qb, kbe, kbo, vbt, gb):
    b, n, _ = qb.shape
    tq = GQA_TQ
    gw = GROUP_B * HEAD_DIM
    assert GQA_UNROLL % 2 == 0 and (n // GQA_TK) % GQA_UNROLL == 0 and n % tq == 0
    rows = pl.BlockSpec((None, n, gw), lambda bi, j: (bi, 0, j))
    keys = pl.BlockSpec((None, None, n, LANES), lambda bi, j: (bi, j, 0, 0))
    return pl.pallas_call(
        _gqa_kernel,
        grid=(b, N_KV_B),
        in_specs=[
            rows, keys, keys,
            pl.BlockSpec((None, None, HEAD_DIM, n), lambda bi, j: (bi, j, 0, 0)),
            rows,
        ],
        out_specs=rows,
        out_shape=jax.ShapeDtypeStruct((b, n, WIDTH_B), BF16),
        scratch_shapes=[
            pltpu.VMEM((GQA_TK, GROUP_B * tq), F32),
            pltpu.VMEM((GQA_TK, GROUP_B * tq), F32),
            pltpu.VMEM((1, GROUP_B * tq), F32),
            pltpu.VMEM((HEAD_DIM + BF16_ROWS, GROUP_B * tq), F32),
        ],
        compiler_params=pltpu.CompilerParams(
            dimension_semantics=("arbitrary", "arbitrary"),
            vmem_limit_bytes=VMEM_LIMIT),
        name="gqa",
    )(qb, kbe, kbo, vbt, gb)


def _dilated_kernel(q_ref, k_ref, v_ref, bias_ref, o_ref, lse_ref, s0_ref, s1_ref):
    dil, length = q_ref.shape[0], q_ref.shape[1]
    tq = DIL_TQ
    nb = length // tq
    steps = dil * nb
    win = min(tq + 2 * DIL_HALF, length)
    n_pairs = N_HEADS_A // 2
    low_lane = lax.broadcasted_iota(jnp.int32, (tq, LANES), 1) < HEAD_DIM
    low_row = lax.broadcasted_iota(jnp.int32, (LANES, tq), 0) < HEAD_DIM
    row8 = lax.broadcasted_iota(jnp.int32, (8, tq), 0)

    def window(t):
        r, i = t // nb, t % nb
        if isinstance(t, int):
            a0 = i * tq
            ws = min(max(a0 - DIL_HALF, 0), length - win)
            table = 0 if i == 0 else (2 if i == nb - 1 else 1)
        else:
            a0 = pl.multiple_of(i * tq, tq)
            ws = pl.multiple_of(jnp.clip(a0 - DIL_HALF, 0, length - win), DIL_HALF)
            table = jnp.where(i == 0, 0, jnp.where(i == nb - 1, 2, 1))
        return r, table, a0, ws

    def scores(t, s_ref):
        r, table, a0, ws = window(t)
        bias = bias_ref[table]
        for pair in range(n_pairs):
            cols = slice(pair * LANES, (pair + 1) * LANES)
            q2 = q_ref[r, pl.ds(a0, tq), cols]
            k2 = k_ref[r, pl.ds(ws, win), cols]
            zero = jnp.zeros_like(q2)
            qs = jnp.concatenate([jnp.where(low_lane, q2, zero),
                                  jnp.where(low_lane, zero, q2)], axis=0)
            s_ref[pair] = _dot_nt(k2, qs) + bias

    def attend(t, s_ref):
        r, _, a0, ws = window(t)
        lse8 = jnp.zeros((8, tq), F32)
        for pair in range(n_pairs):
            cols = slice(pair * LANES, (pair + 1) * LANES)
            v2 = v_ref[r, pl.ds(ws, win), cols]
            s = s_ref[pair]
            m = jnp.max(s, axis=0, keepdims=True)
            p = jnp.exp2(s - m)
            den = jnp.sum(p, axis=0, keepdims=True)
            o_t = lax.dot_general(v2, p.astype(BF16), (((0,), (0,)), ((), ())),
                                  preferred_element_type=F32)
            o_t = o_t * (1.0 / den)
            pair_t = jnp.where(low_row, o_t[:, :tq], o_t[:, tq:])
            o_ref[r, pl.ds(a0, tq), cols] = pair_t.T.astype(BF16)
            lse = (m + jnp.log2(den)) * LN2
            lse8 = jnp.where(row8 == 2 * pair, lse[:, :tq], lse8)
            lse8 = jnp.where(row8 == 2 * pair + 1, lse[:, tq:], lse8)
        lse_t = jnp.concatenate([lse8, jnp.zeros((LANES - 8, tq), F32)], axis=0)
        lse_ref[r, pl.ds(a0, tq), :] = lse_t.T

    scores(0, s0_ref)

    def body(tt, carry):
        t = 2 * tt
        scores(t + 1, s1_ref)
        attend(t, s0_ref)
        scores(t + 2, s0_ref)
        attend(t + 1, s1_ref)
        return carry

    lax.fori_loop(0, steps // 2 - 1, body, 0)
    scores(steps - 1, s1_ref)
    attend(steps - 2, s0_ref)
    attend(steps - 1, s1_ref)


def _dilated(qa, ka, va):
    b, dil, length, _ = qa.shape
    assert (dil * (length // DIL_TQ)) % 2 == 0
    win = min(DIL_TQ + 2 * DIL_HALF, length)
    s_scratch = pltpu.VMEM((N_HEADS_A // 2, win, 2 * DIL_TQ), F32)
    rel = np.arange(win)[:, None] - (np.arange(2 * DIL_TQ) % DIL_TQ)[None, :]
    bias = np.stack([np.where(np.abs(rel + off) <= DIL_HALF, 0.0, NEG_BIG)
                     for off in (0, -DIL_HALF, DIL_TQ - win)]).astype(np.float32)
    bias_spec = pl.BlockSpec(bias.shape, lambda bi: (0, 0, 0))
    spec = pl.BlockSpec((None, dil, length, WIDTH_A), lambda bi: (bi, 0, 0, 0))
    lse_spec = pl.BlockSpec((None, dil, length, LANES), lambda bi: (bi, 0, 0, 0))
    return pl.pallas_call(
        _dilated_kernel,
        grid=(b,),
        in_specs=[spec, spec, spec, bias_spec],
        out_specs=(spec, lse_spec),
        out_shape=(jax.ShapeDtypeStruct((b, dil, length, WIDTH_A), BF16),
                   jax.ShapeDtypeStruct((b, dil, length, LANES), F32)),
        scratch_shapes=[s_scratch, s_scratch],
        compiler_params=pltpu.CompilerParams(
            dimension_semantics=("arbitrary",), vmem_limit_bytes=VMEM_LIMIT),
        name=f"dilated{dil}",
    )(qa, ka, va, jnp.asarray(bias))


def _outproj_kernel(x_ref, o1_ref, o2_ref, o3_ref, l1_ref, l2_ref, l3_ref, ga_ref, mb_ref,
                    w_ref, ex_ref, lg_ref, lb_ref, y_ref, o_scr, l_scr, mix_scr):
    tm = x_ref.shape[0]
    n_slabs = WIDTH_A // LANES
    rc = OUT_ROW_CHUNK

    for slot, (o_ref, l_ref) in enumerate(((o2_ref, l2_ref), (o3_ref, l3_ref))):
        dil = o_ref.shape[0]
        for r in range(dil):
            for c in range(n_slabs):
                part = o_ref[r, :, c * LANES:(c + 1) * LANES].astype(F32)
                o_scr[slot * n_slabs + c, pl.ds(r, tm // dil, stride=dil), :] = part
            l_scr[slot, pl.ds(r, tm // dil, stride=dil), :] = l_ref[r]

    ex = ex_ref[...]
    for ch in range(tm // rc):
        rows = slice(ch * rc, (ch + 1) * rc)
        lses = (l1_ref[0, rows, :], l_scr[0, rows, :], l_scr[1, rows, :])
        mx = jnp.maximum(jnp.maximum(lses[0], lses[1]), lses[2])
        es = [jnp.exp(l - mx) for l in lses]
        inv = 1.0 / (es[0] + es[1] + es[2])
        o1 = o1_ref[0, rows, :].astype(F32)
        out_a = o1
        for slot in range(2):
            w = _dot((es[slot + 1] * inv).astype(BF16), ex)
            o = jnp.concatenate([o_scr[slot * n_slabs + c, rows, :] for c in range(n_slabs)], axis=1)
            out_a = out_a + w * (o - o1)
        mix_scr[rows, :] = (out_a * ga_ref[rows, :].astype(F32)).astype(BF16)

    f = _dot(mix_scr[...], w_ref[:WIDTH_A, :]) + _dot(mb_ref[...], w_ref[WIDTH_A:, :])
    for ch in range(tm // rc):
        rows = slice(ch * rc, (ch + 1) * rc)
        z = DEEPNORM_ALPHA * x_ref[rows, :] + f[rows, :]
        mu = jnp.mean(z, axis=-1, keepdims=True)
        zc = z - mu
        var = jnp.mean(zc * zc, axis=-1, keepdims=True)
        y_ref[rows, :] = zc * lax.rsqrt(var + LN_EPS) * lg_ref[...] + lb_ref[...]


def _outproj(x2, o_list, lse_list, ga, mix_b, w_out_bf16, expand, ln_g, ln_b):
    t = x2.shape[0]
    tm = TOKEN_TILE
    nt = o_list[0].shape[1] * o_list[0].shape[2] // tm
    row = lambda i: (i, 0)
    const = lambda i: (0, 0)
    wide = lambda w: pl.BlockSpec((tm, w), row)

    def by_residue(a):
        _, dil, _, width = a.shape
        return pl.BlockSpec((None, dil, tm // dil, width), lambda i: (i // nt, 0, i % nt, 0))

    n_strided = sum(o.shape[1] > 1 for o in o_list)
    return pl.pallas_call(
        _outproj_kernel,
        grid=(t // tm,),
        in_specs=[
            wide(D_MODEL),
            *[by_residue(o) for o in o_list],
            *[by_residue(l) for l in lse_list],
            wide(WIDTH_A), wide(WIDTH_B),
            pl.BlockSpec((WIDTH_A + WIDTH_B, D_MODEL), const),
            pl.BlockSpec((LANES, WIDTH_A), const),
            pl.BlockSpec((1, D_MODEL), const),
            pl.BlockSpec((1, D_MODEL), const),
        ],
        out_specs=wide(D_MODEL),
        out_shape=jax.ShapeDtypeStruct((t, D_MODEL), F32),
        scratch_shapes=[pltpu.VMEM((n_strided * WIDTH_A // LANES, tm, LANES), F32),
                        pltpu.VMEM((n_strided, tm, LANES), F32),
                        pltpu.VMEM((tm, WIDTH_A), BF16)],
        compiler_params=pltpu.CompilerParams(
            dimension_semantics=("arbitrary",), vmem_limit_bytes=VMEM_LIMIT),
        name="outproj",
    )(x2, *o_list, *lse_list, ga, mix_b, w_out_bf16, expand, ln_g, ln_b)


def _constants():
    head_of_lane = np.arange(LANES) // HEAD_DIM
    bd = (head_of_lane[:, None] == head_of_lane[None, :]).astype(np.float32)
    ex = (np.arange(LANES)[:, None] == (np.arange(WIDTH_A) // HEAD_DIM)[None, :]).astype(np.float32)
    return jnp.asarray(bd, BF16), jnp.asarray(ex, BF16)


def _layer(x, w_in_bf16, w_out_bf16, tables, qn, kn, ln_g, ln_b, bd, expand):
    b, n, _ = x.shape
    (qa, ka, va, qa4, ka4, va4, qa16, ka16, va16,
     ga, qb, gb, kbe, kbo, vbt) = _inproj(x, w_in_bf16, tables, qn, kn, bd)
    mix_b = _gqa(qb.reshape(b, n, WIDTH_B), kbe, kbo, vbt, gb.reshape(b, n, WIDTH_B))
    assert tuple(d for _, d in DILATED_PATTERNS) == (1, 4, 16)
    one = lambda a: a.reshape(b, 1, n, WIDTH_A)
    o_list, lse_list = [], []
    for qkv in ((one(qa), one(ka), one(va)), (qa4, ka4, va4), (qa16, ka16, va16)):
        o, lse = _dilated(*qkv)
        o_list.append(o)
        lse_list.append(lse)
    y = _outproj(x.reshape(b * n, D_MODEL), o_list, lse_list, ga, mix_b.reshape(b * n, WIDTH_B),
                 w_out_bf16, expand, ln_g, ln_b)
    return y.reshape(b, n, D_MODEL)


def kernel(x_prompt, x_sample, w_in, w_out, q_norm, k_norm, ln_g, ln_b):
    bd, expand = _constants()
    tables = _rope_tables(max(x_prompt.shape[1], x_sample.shape[1]))
    rep = LANES // HEAD_DIM
    y_prompt, y_sample = x_prompt, x_sample
    for i in range(w_in.shape[0]):
        w_in_bf16 = w_in[i].astype(BF16)
        w_out_bf16 = w_out[i].astype(BF16)
        qn = jnp.tile(q_norm[i].reshape(1, HEAD_DIM), (1, rep))
        kn = jnp.tile(k_norm[i].reshape(1, HEAD_DIM), (1, rep))
        args = (w_in_bf16, w_out_bf16, tables, qn, kn,
                ln_g[i].reshape(1, D_MODEL), ln_b[i].reshape(1, D_MODEL), bd, expand)
        y_prompt = _layer(y_prompt, *args)
        y_sample = _layer(y_sample, *args)
    return (y_prompt, y_sample)
```

```python
import functools
import math

import numpy as np
import jax
import jax.numpy as jnp
from jax import lax
from jax.experimental import pallas as pl
from jax.experimental.pallas import tpu as pltpu

D_MODEL = 1024
HEAD_DIM = 64
N_HEADS_A = 8
N_HEADS_B = 8
N_KV_B = 2
GROUP_B = N_HEADS_B // N_KV_B
WIDTH_A = N_HEADS_A * HEAD_DIM
WIDTH_B = N_HEADS_B * HEAD_DIM
KV_WIDTH_B = N_KV_B * HEAD_DIM
DILATED_PATTERNS = ((128, 1), (512, 4), (2048, 16))
GRID_W = 64
ROPE_THETA = 10000.0
LN_EPS = 1e-5
RMS_EPS = 1e-6
NEG_BIG = -1e30
DEPTH = 1
DEEPNORM_ALPHA = (2.0 * DEPTH) ** 0.25

OFF_QA = 0
OFF_KA = OFF_QA + WIDTH_A
OFF_VA = OFF_KA + WIDTH_A
OFF_GA = OFF_VA + WIDTH_A
OFF_QB = OFF_GA + WIDTH_A
OFF_KB = OFF_QB + WIDTH_B
OFF_VB = OFF_KB + KV_WIDTH_B
OFF_GB = OFF_VB + KV_WIDTH_B
IN_WIDTH = OFF_GB + WIDTH_B

LANES = 128
LOG2E = 1.4426950408889634
LN2 = 0.6931471805599453
SCORE_SCALE = (HEAD_DIM ** -0.5) * LOG2E

TOKEN_TILE = 512
GQA_TQ = 256
GQA_TK = 512
GQA_UNROLL = 4
BF16_ROWS = 16
OUT_ROW_CHUNK = 128
DIL_TQ = 128
DIL_HALF = 64
VMEM_LIMIT = 48 * 1024 * 1024

F32 = jnp.float32
BF16 = jnp.bfloat16


def _dot(a, b):
    return jnp.dot(a, b, preferred_element_type=F32)


def _dot_nt(a, b):
    return lax.dot_general(a, b, (((1,), (1,)), ((), ())), preferred_element_type=F32)


def _split_dot(x, w_bf16):
    hi = x.astype(BF16)
    lo = (x - hi.astype(F32)).astype(BF16)
    return _dot(hi, w_bf16) + _dot(lo, w_bf16)


def _rotate_half(x, first, half):
    return jnp.where(first, pltpu.roll(x, LANES - half, 1), pltpu.roll(x, half, 1))


def _inproj_kernel(x_ref, w_ref, ca_ref, sa_ref, cb_ref, sb_ref, qn_ref, kn_ref, bd_ref,
                   qa_ref, ka_ref, va_ref, qa4_ref, ka4_ref, va4_ref, qa16_ref, ka16_ref, va16_ref,
                   ga_ref, qb_ref, gb_ref, kbe_ref, kbo_ref, vbt_ref, slab_ref, slab4_ref):
    tm = x_ref.shape[0]
    xb = x_ref[...].astype(BF16)

    def proj(off, width):
        return _dot(xb, w_ref[:, off:off + width])

    lane = lax.broadcasted_iota(jnp.int32, (tm, LANES), 1)
    first_a = (lane % HEAD_DIM) < (HEAD_DIM // 2)
    first_b = (lane % (HEAD_DIM // 2)) < (HEAD_DIM // 4)
    ca, sa = ca_ref[...], sa_ref[...]
    cb, sb = cb_ref[...], sb_ref[...]
    bd = bd_ref[...]

    def rope_a(x):
        return x * ca + _rotate_half(x, first_a, HEAD_DIM // 2) * sa

    def norm_rope_b(x, gain):
        ms = _split_dot(x * x, bd) * (1.0 / HEAD_DIM)
        xn = x * lax.rsqrt(ms + RMS_EPS) * gain
        return xn * cb + _rotate_half(xn, first_b, HEAD_DIM // 4) * sb

    def silu(g):
        return g * (1.0 / (1.0 + jnp.exp(-g)))

    def kv_b(h):
        kb = norm_rope_b(h[:, :KV_WIDTH_B], kn_ref[...])
        kb_swapped = pltpu.roll(kb, HEAD_DIM, 1)
        low = lane < HEAD_DIM
        zero = jnp.zeros_like(kb)
        kbe_ref[0] = jnp.where(low, kb, zero).astype(BF16)
        kbo_ref[0] = jnp.where(low, zero, kb_swapped).astype(BF16)
        kbe_ref[1] = jnp.where(low, kb_swapped, zero).astype(BF16)
        kbo_ref[1] = jnp.where(low, zero, kb).astype(BF16)
        vbt = h[:, KV_WIDTH_B:].T
        for j in range(N_KV_B):
            vbt_ref[j] = vbt[j * HEAD_DIM:(j + 1) * HEAD_DIM, :].astype(BF16)

    def q_b(h):
        qn = qn_ref[...]
        for c in range(WIDTH_B // LANES):
            qb = norm_rope_b(h[:, c * LANES:(c + 1) * LANES], qn) * SCORE_SCALE
            qb_ref[:, c * LANES:(c + 1) * LANES] = qb.astype(BF16)

    n_slabs = WIDTH_A // LANES
    finish = (lambda x: rope_a(x) * SCORE_SCALE, rope_a, lambda x: x)
    segments = ((OFF_QA, qa_ref, qa4_ref, qa16_ref), (OFF_KA, ka_ref, ka4_ref, ka16_ref),
                (OFF_VA, va_ref, va4_ref, va16_ref))

    def qkv_a(which, h):
        _, nat_ref, ref4, ref16 = segments[which]
        for c in range(n_slabs):
            cols = slice(c * LANES, (c + 1) * LANES)
            val = finish[which](h[:, cols])
            nat_ref[:, cols] = val.astype(BF16)
            slab_ref[which * n_slabs + c] = val
        for c in range(n_slabs):
            cols = slice(c * LANES, (c + 1) * LANES)
            base = (which * n_slabs + c) * 4
            for r4 in range(4):
                part = slab_ref[which * n_slabs + c, pl.ds(r4, tm // 4, stride=4), :]
                ref4[r4, :, cols] = part.astype(BF16)
                slab4_ref[base + r4] = part
            for r16 in range(16):
                part = slab4_ref[base + r16 % 4, pl.ds(r16 // 4, tm // 16, stride=4), :]
                ref16[r16, :, cols] = part.astype(BF16)

    def gate(ref):
        def epilogue(h):
            ref[...] = silu(h).astype(BF16)
        return epilogue

    plan = (
        (OFF_QA, WIDTH_A, functools.partial(qkv_a, 0)),
        (OFF_KB, 2 * KV_WIDTH_B, kv_b),
        (OFF_KA, WIDTH_A, functools.partial(qkv_a, 1)),
        (OFF_QB, WIDTH_B, q_b),
        (OFF_VA, WIDTH_A, functools.partial(qkv_a, 2)),
        (OFF_GA, WIDTH_A, gate(ga_ref)),
        (OFF_GB, WIDTH_B, gate(gb_ref)),
    )
    h = proj(plan[0][0], plan[0][1])
    for i, (_, _, epilogue) in enumerate(plan):
        h_next = proj(plan[i + 1][0], plan[i + 1][1]) if i + 1 < len(plan) else None
        epilogue(h)
        h = h_next


def _rope_tables(n):
    def cos_sin(pos, dim):
        inv_freq = 1.0 / (ROPE_THETA ** (jnp.arange(0, dim, 2, dtype=F32) / dim))
        ang = pos.astype(F32)[:, None] * inv_freq[None, :]
        ang = jnp.concatenate([ang, ang], axis=-1)
        sign = jnp.concatenate([-jnp.ones((dim // 2,), F32), jnp.ones((dim // 2,), F32)])
        return jnp.cos(ang), jnp.sin(ang) * sign[None, :]

    pos = jnp.arange(n, dtype=jnp.int32)
    ca, sa = cos_sin(pos, HEAD_DIM)
    row = pos // GRID_W
    col = pos % GRID_W
    cr, sr = cos_sin(row, HEAD_DIM // 2)
    cc, sc = cos_sin(col, HEAD_DIM // 2)
    cb = jnp.concatenate([cr, cc], axis=-1)
    sb = jnp.concatenate([sr, sc], axis=-1)
    rep = LANES // HEAD_DIM
    return tuple(jnp.tile(t, (1, rep)) for t in (ca, sa, cb, sb))


def _inproj(x, w_bf16, tables, qn, kn, bd):
    b, n, _ = x.shape
    tm = TOKEN_TILE
    nt = n // tm
    t = b * n
    x2 = x.reshape(t, D_MODEL)
    row = lambda i: (i, 0)
    pos = lambda i: (i % nt, 0)
    const = lambda i: (0, 0)
    tab_spec = pl.BlockSpec((tm, LANES), pos)
    wide = lambda w: pl.BlockSpec((tm, w), row)
    def by_residue(dil):
        shape = jax.ShapeDtypeStruct((b, dil, n // dil, WIDTH_A), BF16)
        spec = pl.BlockSpec((None, dil, tm // dil, WIDTH_A), lambda i: (i // nt, 0, i % nt, 0))
        return [shape] * 3, [spec] * 3

    shapes4, specs4 = by_residue(4)
    shapes16, specs16 = by_residue(16)
    out_shapes = (
        jax.ShapeDtypeStruct((t, WIDTH_A), BF16),
        jax.ShapeDtypeStruct((t, WIDTH_A), BF16),
        jax.ShapeDtypeStruct((t, WIDTH_A), BF16),
        *shapes4, *shapes16,
        jax.ShapeDtypeStruct((t, WIDTH_A), BF16),
        jax.ShapeDtypeStruct((t, WIDTH_B), BF16),
        jax.ShapeDtypeStruct((t, WIDTH_B), BF16),
        jax.ShapeDtypeStruct((b, N_KV_B, n, LANES), BF16),
        jax.ShapeDtypeStruct((b, N_KV_B, n, LANES), BF16),
        jax.ShapeDtypeStruct((b, N_KV_B, HEAD_DIM, n), BF16),
    )
    out_specs = (
        wide(WIDTH_A), wide(WIDTH_A), wide(WIDTH_A), *specs4, *specs16,
        wide(WIDTH_A), wide(WIDTH_B), wide(WIDTH_B),
        pl.BlockSpec((None, N_KV_B, tm, LANES), lambda i: (i // nt, 0, i % nt, 0)),
        pl.BlockSpec((None, N_KV_B, tm, LANES), lambda i: (i // nt, 0, i % nt, 0)),
        pl.BlockSpec((None, N_KV_B, HEAD_DIM, tm), lambda i: (i // nt, 0, 0, i % nt)),
    )
    in_specs = [
        pl.BlockSpec((tm, D_MODEL), row),
        pl.BlockSpec((D_MODEL, IN_WIDTH), const),
        tab_spec, tab_spec, tab_spec, tab_spec,
        pl.BlockSpec((1, LANES), const),
        pl.BlockSpec((1, LANES), const),
        pl.BlockSpec((LANES, LANES), const),
    ]
    return pl.pallas_call(
        _inproj_kernel,
        grid=(t // tm,),
        in_specs=in_specs,
        out_specs=out_specs,
        out_shape=out_shapes,
        scratch_shapes=[pltpu.VMEM((3 * WIDTH_A // LANES, tm, LANES), F32),
                        pltpu.VMEM((4 * 3 * WIDTH_A // LANES, tm // 4, LANES), F32)],
        compiler_params=pltpu.CompilerParams(
            dimension_semantics=("arbitrary",), vmem_limit_bytes=VMEM_LIMIT),
        name="inproj",
    )(x2, w_bf16, *tables, qn, kn, bd)


def _gqa_kernel(q_ref, ke_ref, ko_ref, vt_ref, g_ref, o_ref, s0_ref, s1_ref, m_ref, acc_ref):
    n = ke_ref.shape[0]
    tq, tk = GQA_TQ, GQA_TK
    nc = n // tk
    steps = (n // tq) * nc
    m_ref[...] = jnp.full(m_ref.shape, NEG_BIG, F32)
    acc_ref[...] = jnp.zeros(acc_ref.shape, F32)
    ones = jnp.ones((BF16_ROWS, tk), BF16)

    def offsets(t):
        qi, c = t // nc, t % nc
        if isinstance(t, int):
            return qi * tq, c * tk, c
        return pl.multiple_of(qi * tq, tq), pl.multiple_of(c * tk, tk), c

    def scores(t, s_ref):
        qoff, koff, _ = offsets(t)
        for pr in range(GROUP_B // 2):
            q_pair = q_ref[pl.ds(qoff, tq), pr * LANES:(pr + 1) * LANES]
            for odd, k_ref in enumerate((ke_ref, ko_ref)):
                h = 2 * pr + odd
                s_ref[:, h * tq:(h + 1) * tq] = _dot_nt(k_ref[pl.ds(koff, tk), :], q_pair)

    def update(t, s_ref):
        _, koff, c = offsets(t)
        s = s_ref[...]
        m_prev = jnp.where(c == 0, NEG_BIG, m_ref[...])
        m_new = jnp.maximum(m_prev, jnp.max(s, axis=0, keepdims=True))
        alpha = jnp.exp2(m_prev - m_new)
        p = jnp.exp2(s - m_new).astype(BF16)
        vt = jnp.concatenate([vt_ref[:, pl.ds(koff, tk)], ones], axis=0)
        acc_ref[...] = alpha * acc_ref[...] + _dot(vt, p)
        m_ref[...] = m_new

    def finalize(t):
        qoff, _, _ = offsets(t)
        acc = acc_ref[...]
        o_t = acc[:HEAD_DIM] * (1.0 / acc[HEAD_DIM:HEAD_DIM + 1])
        for pr in range(GROUP_B // 2):
            pair_t = jnp.concatenate([o_t[:, (2 * pr) * tq:(2 * pr + 1) * tq],
                                      o_t[:, (2 * pr + 1) * tq:(2 * pr + 2) * tq]], axis=0)
            gate = g_ref[pl.ds(qoff, tq), pr * LANES:(pr + 1) * LANES].astype(F32)
            o_ref[pl.ds(qoff, tq), pr * LANES:(pr + 1) * LANES] = (pair_t.T * gate).astype(BF16)

    unroll = GQA_UNROLL
    bufs = (s0_ref, s1_ref)
    scores(0, bufs[0])

    def body(tt, carry):
        t = unroll * tt
        for u in range(unroll):
            scores(t + u + 1, bufs[(u + 1) % 2])
            update(t + u, bufs[u % 2])

        @pl.when((t + unroll - 1) % nc == nc - 1)
        def _():
            finalize(t + unroll - 1)

        return carry

    lax.fori_loop(0, steps // unroll - 1, body, 0)
    for u in range(unroll):
        t = steps - unroll + u
        if u < unroll - 1:
            scores(t + 1, bufs[(u + 1) % 2])
        update(t, bufs[u % 2])
    finalize(steps - 1)


def _gqa(qb, kbe, kbo, vbt, gb):
    b, n, _ = qb.shape
    tq = GQA_TQ
    gw = GROUP_B * HEAD_DIM
    assert GQA_UNROLL % 2 == 0 and (n // GQA_TK) % GQA_UNROLL == 0 and n % tq == 0
    rows = pl.BlockSpec((None, n, gw), lambda bi, j: (bi, 0, j))
    keys = pl.BlockSpec((None, None, n, LANES), lambda bi, j: (bi, j, 0, 0))
    return pl.pallas_call(
        _gqa_kernel,
        grid=(b, N_KV_B),
        in_specs=[
            rows, keys, keys,
            pl.BlockSpec((None, None, HEAD_DIM, n), lambda bi, j: (bi, j, 0, 0)),
            rows,
        ],
        out_specs=rows,
        out_shape=jax.ShapeDtypeStruct((b, n, WIDTH_B), BF16),
        scratch_shapes=[
            pltpu.VMEM((GQA_TK, GROUP_B * tq), F32),
            pltpu.VMEM((GQA_TK, GROUP_B * tq), F32),
            pltpu.VMEM((1, GROUP_B * tq), F32),
            pltpu.VMEM((HEAD_DIM + BF16_ROWS, GROUP_B * tq), F32),
        ],
        compiler_params=pltpu.CompilerParams(
            dimension_semantics=("arbitrary", "arbitrary"),
            vmem_limit_bytes=VMEM_LIMIT),
        name="gqa",
    )(qb, kbe, kbo, vbt, gb)


def _dilated_kernel(q_ref, k_ref, v_ref, bias_ref, o_ref, lse_ref, s0_ref, s1_ref):
    dil, length = q_ref.shape[0], q_ref.shape[1]
    tq = DIL_TQ
    nb = length // tq
    steps = dil * nb
    win = min(tq + 2 * DIL_HALF, length)
    n_pairs = N_HEADS_A // 2
    low_lane = lax.broadcasted_iota(jnp.int32, (tq, LANES), 1) < HEAD_DIM
    low_row = lax.broadcasted_iota(jnp.int32, (LANES, tq), 0) < HEAD_DIM
    row8 = lax.broadcasted_iota(jnp.int32, (8, tq), 0)

    def window(t):
        r, i = t // nb, t % nb
        if isinstance(t, int):
            a0 = i * tq
            ws = min(max(a0 - DIL_HALF, 0), length - win)
            table = 0 if i == 0 else (2 if i == nb - 1 else 1)
        else:
            a0 = pl.multiple_of(i * tq, tq)
            ws = pl.multiple_of(jnp.clip(a0 - DIL_HALF, 0, length - win), DIL_HALF)
            table = jnp.where(i == 0, 0, jnp.where(i == nb - 1, 2, 1))
        return r, table, a0, ws

    def scores(t, s_ref):
        r, table, a0, ws = window(t)
        bias = bias_ref[table]
        for pair in range(n_pairs):
            cols = slice(pair * LANES, (pair + 1) * LANES)
            q2 = q_ref[r, pl.ds(a0, tq), cols]
            k2 = k_ref[r, pl.ds(ws, win), cols]
            zero = jnp.zeros_like(q2)
            qs = jnp.concatenate([jnp.where(low_lane, q2, zero),
                                  jnp.where(low_lane, zero, q2)], axis=0)
            s_ref[pair] = _dot_nt(k2, qs) + bias

    def attend(t, s_ref):
        r, _, a0, ws = window(t)
        lse8 = jnp.zeros((8, tq), F32)
        for pair in range(n_pairs):
            cols = slice(pair * LANES, (pair + 1) * LANES)
            v2 = v_ref[r, pl.ds(ws, win), cols]
            s = s_ref[pair]
            m = jnp.max(s, axis=0, keepdims=True)
            p = jnp.exp2(s - m)
            den = jnp.sum(p, axis=0, keepdims=True)
            o_t = lax.dot_general(v2, p.astype(BF16), (((0,), (0,)), ((), ())),
                                  preferred_element_type=F32)
            o_t = o_t * (1.0 / den)
            pair_t = jnp.where(low_row, o_t[:, :tq], o_t[:, tq:])
            o_ref[r, pl.ds(a0, tq), cols] = pair_t.T.astype(BF16)
            lse = (m + jnp.log2(den)) * LN2
            lse8 = jnp.where(row8 == 2 * pair, lse[:, :tq], lse8)
            lse8 = jnp.where(row8 == 2 * pair + 1, lse[:, tq:], lse8)
        lse_t = jnp.concatenate([lse8, jnp.zeros((LANES - 8, tq), F32)], axis=0)
        lse_ref[r, pl.ds(a0, tq), :] = lse_t.T

    scores(0, s0_ref)

    def body(tt, carry):
        t = 2 * tt
        scores(t + 1, s1_ref)
        attend(t, s0_ref)
        scores(t + 2, s0_ref)
        attend(t + 1, s1_ref)
        return carry

    lax.fori_loop(0, steps // 2 - 1, body, 0)
    scores(steps - 1, s1_ref)
    attend(steps - 2, s0_ref)
    attend(steps - 1, s1_ref)


def _dilated(qa, ka, va):
    b, dil, length, _ = qa.shape
    assert (dil * (length // DIL_TQ)) % 2 == 0
    win = min(DIL_TQ + 2 * DIL_HALF, length)
    s_scratch = pltpu.VMEM((N_HEADS_A // 2, win, 2 * DIL_TQ), F32)
    rel = np.arange(win)[:, None] - (np.arange(2 * DIL_TQ) % DIL_TQ)[None, :]
    bias = np.stack([np.where(np.abs(rel + off) <= DIL_HALF, 0.0, NEG_BIG)
                     for off in (0, -DIL_HALF, DIL_TQ - win)]).astype(np.float32)
    bias_spec = pl.BlockSpec(bias.shape, lambda bi: (0, 0, 0))
    spec = pl.BlockSpec((None, dil, length, WIDTH_A), lambda bi: (bi, 0, 0, 0))
    lse_spec = pl.BlockSpec((None, dil, length, LANES), lambda bi: (bi, 0, 0, 0))
    return pl.pallas_call(
        _dilated_kernel,
        grid=(b,),
        in_specs=[spec, spec, spec, bias_spec],
        out_specs=(spec, lse_spec),
        out_shape=(jax.ShapeDtypeStruct((b, dil, length, WIDTH_A), BF16),
                   jax.ShapeDtypeStruct((b, dil, length, LANES), F32)),
        scratch_shapes=[s_scratch, s_scratch],
        compiler_params=pltpu.CompilerParams(
            dimension_semantics=("arbitrary",), vmem_limit_bytes=VMEM_LIMIT),
        name=f"dilated{dil}",
    )(qa, ka, va, jnp.asarray(bias))


def _outproj_kernel(x_ref, o1_ref, o2_ref, o3_ref, l1_ref, l2_ref, l3_ref, ga_ref, mb_ref,
                    w_ref, ex_ref, lg_ref, lb_ref, y_ref, o_scr, l_scr, mix_scr):
    tm = x_ref.shape[0]
    n_slabs = WIDTH_A // LANES
    rc = OUT_ROW_CHUNK

    for slot, (o_ref, l_ref) in enumerate(((o2_ref, l2_ref), (o3_ref, l3_ref))):
        dil = o_ref.shape[0]
        for r in range(dil):
            for c in range(n_slabs):
                part = o_ref[r, :, c * LANES:(c + 1) * LANES].astype(F32)
                o_scr[slot * n_slabs + c, pl.ds(r, tm // dil, stride=dil), :] = part
            l_scr[slot, pl.ds(r, tm // dil, stride=dil), :] = l_ref[r]

    ex = ex_ref[...]

    def mix(rows):
        lses = (l1_ref[0, rows, :], l_scr[0, rows, :], l_scr[1, rows, :])
        mx = jnp.maximum(jnp.maximum(lses[0], lses[1]), lses[2])
        es = [jnp.exp(l - mx) for l in lses]
        inv = 1.0 / (es[0] + es[1] + es[2])
        o1 = o1_ref[0, rows, :].astype(F32)
        out_a = o1
        for slot in range(2):
            w = _dot((es[slot + 1] * inv).astype(BF16), ex)
            o = jnp.concatenate([o_scr[slot * n_slabs + c, rows, :] for c in range(n_slabs)], axis=1)
            out_a = out_a + w * (o - o1)
        return (out_a * ga_ref[rows, :].astype(F32)).astype(BF16)

    def norm(rows, f):
        z = DEEPNORM_ALPHA * x_ref[rows, :] + f
        mu = jnp.mean(z, axis=-1, keepdims=True)
        zc = z - mu
        var = jnp.mean(zc * zc, axis=-1, keepdims=True)
        y_ref[rows, :] = zc * lax.rsqrt(var + LN_EPS) * lg_ref[...] + lb_ref[...]

    chunks = [slice(ch * rc, (ch + 1) * rc) for ch in range(tm // rc)]
    for rows in chunks:
        mix_scr[rows, :] = mix(rows)
    f = _dot(mix_scr[...], w_ref[:WIDTH_A, :]) + _dot(mb_ref[...], w_ref[WIDTH_A:, :])
    for rows in chunks:
        norm(rows, f[rows, :])


def _outproj(x2, o_list, lse_list, ga, mix_b, w_out_bf16, expand, ln_g, ln_b):
    t = x2.shape[0]
    tm = TOKEN_TILE
    nt = o_list[0].shape[1] * o_list[0].shape[2] // tm
    row = lambda i: (i, 0)
    const = lambda i: (0, 0)
    wide = lambda w: pl.BlockSpec((tm, w), row)

    def by_residue(a):
        _, dil, _, width = a.shape
        return pl.BlockSpec((None, dil, tm // dil, width), lambda i: (i // nt, 0, i % nt, 0))

    n_strided = sum(o.shape[1] > 1 for o in o_list)
    return pl.pallas_call(
        _outproj_kernel,
        grid=(t // tm,),
        in_specs=[
            wide(D_MODEL),
            *[by_residue(o) for o in o_list],
            *[by_residue(l) for l in lse_list],
            wide(WIDTH_A), wide(WIDTH_B),
            pl.BlockSpec((WIDTH_A + WIDTH_B, D_MODEL), const),
            pl.BlockSpec((LANES, WIDTH_A), const),
            pl.BlockSpec((1, D_MODEL), const),
            pl.BlockSpec((1, D_MODEL), const),
        ],
        out_specs=wide(D_MODEL),
        out_shape=jax.ShapeDtypeStruct((t, D_MODEL), F32),
        scratch_shapes=[pltpu.VMEM((n_strided * WIDTH_A // LANES, tm, LANES), F32),
                        pltpu.VMEM((n_strided, tm, LANES), F32),
                        pltpu.VMEM((tm, WIDTH_A), BF16)],
        compiler_params=pltpu.CompilerParams(
            dimension_semantics=("arbitrary",), vmem_limit_bytes=VMEM_LIMIT),
        name="outproj",
    )(x2, *o_list, *lse_list, ga, mix_b, w_out_bf16, expand, ln_g, ln_b)


def _constants():
    head_of_lane = np.arange(LANES) // HEAD_DIM
    bd = (head_of_lane[:, None] == head_of_lane[None, :]).astype(np.float32)
    ex = (np.arange(LANES)[:, None] == (np.arange(WIDTH_A) // HEAD_DIM)[None, :]).astype(np.float32)
    return jnp.asarray(bd, BF16), jnp.asarray(ex, BF16)


def _layer(x, w_in_bf16, w_out_bf16, tables, qn, kn, ln_g, ln_b, bd, expand):
    b, n, _ = x.shape
    (qa, ka, va, qa4, ka4, va4, qa16, ka16, va16,
     ga, qb, gb, kbe, kbo, vbt) = _inproj(x, w_in_bf16, tables, qn, kn, bd)
    mix_b = _gqa(qb.reshape(b, n, WIDTH_B), kbe, kbo, vbt, gb.reshape(b, n, WIDTH_B))
    assert tuple(d for _, d in DILATED_PATTERNS) == (1, 4, 16)
    one = lambda a: a.reshape(b, 1, n, WIDTH_A)
    o_list, lse_list = [], []
    for qkv in ((one(qa), one(ka), one(va)), (qa4, ka4, va4), (qa16, ka16, va16)):
        o, lse = _dilated(*qkv)
        o_list.append(o)
        lse_list.append(lse)
    y = _outproj(x.reshape(b * n, D_MODEL), o_list, lse_list, ga, mix_b.reshape(b * n, WIDTH_B),
                 w_out_bf16, expand, ln_g, ln_b)
    return y.reshape(b, n, D_MODEL)


def kernel(x_prompt, x_sample, w_in, w_out, q_norm, k_norm, ln_g, ln_b):
    bd, expand = _constants()
    tables = _rope_tables(max(x_prompt.shape[1], x_sample.shape[1]))
    rep = LANES // HEAD_DIM
    y_prompt, y_sample = x_prompt, x_sample
    for i in range(w_in.shape[0]):
        w_in_bf16 = w_in[i].astype(BF16)
        w_out_bf16 = w_out[i].astype(BF16)
        qn = jnp.tile(q_norm[i].reshape(1, HEAD_DIM), (1, rep))
        kn = jnp.tile(k_norm[i].reshape(1, HEAD_DIM), (1, rep))
        args = (w_in_bf16, w_out_bf16, tables, qn, kn,
                ln_g[i].reshape(1, D_MODEL), ln_b[i].reshape(1, D_MODEL), bd, expand)
        y_prompt = _layer(y_prompt, *args)
        y_sample = _layer(y_sample, *args)
    return (y_prompt, y_sample)
```

```python
import functools
import math

import numpy as np
import jax
import jax.numpy as jnp
from jax import lax
from jax.experimental import pallas as pl
from jax.experimental.pallas import tpu as pltpu

D_MODEL = 1024
HEAD_DIM = 64
N_HEADS_A = 8
N_HEADS_B = 8
N_KV_B = 2
GROUP_B = N_HEADS_B // N_KV_B
WIDTH_A = N_HEADS_A * HEAD_DIM
WIDTH_B = N_HEADS_B * HEAD_DIM
KV_WIDTH_B = N_KV_B * HEAD_DIM
DILATED_PATTERNS = ((128, 1), (512, 4), (2048, 16))
GRID_W = 64
ROPE_THETA = 10000.0
LN_EPS = 1e-5
RMS_EPS = 1e-6
NEG_BIG = -1e30
DEPTH = 1
DEEPNORM_ALPHA = (2.0 * DEPTH) ** 0.25

OFF_QA = 0
OFF_KA = OFF_QA + WIDTH_A
OFF_VA = OFF_KA + WIDTH_A
OFF_GA = OFF_VA + WIDTH_A
OFF_QB = OFF_GA + WIDTH_A
OFF_KB = OFF_QB + WIDTH_B
OFF_VB = OFF_KB + KV_WIDTH_B
OFF_GB = OFF_VB + KV_WIDTH_B
IN_WIDTH = OFF_GB + WIDTH_B

LANES = 128
LOG2E = 1.4426950408889634
LN2 = 0.6931471805599453
SCORE_SCALE = (HEAD_DIM ** -0.5) * LOG2E

TOKEN_TILE = 512
GQA_TQ = 256
GQA_TK = 512
GQA_MAX_UNROLL = 8
BF16_ROWS = 16
OUT_ROW_CHUNK = 128
DIL_TQ = 128
DIL_HALF = 64
VMEM_LIMIT = 48 * 1024 * 1024

F32 = jnp.float32
BF16 = jnp.bfloat16


def _dot(a, b):
    return jnp.dot(a, b, preferred_element_type=F32)


def _dot_nt(a, b):
    return lax.dot_general(a, b, (((1,), (1,)), ((), ())), preferred_element_type=F32)


def _split_dot(x, w_bf16):
    hi = x.astype(BF16)
    lo = (x - hi.astype(F32)).astype(BF16)
    return _dot(hi, w_bf16) + _dot(lo, w_bf16)


def _rotate_half(x, first, half):
    return jnp.where(first, pltpu.roll(x, LANES - half, 1), pltpu.roll(x, half, 1))


def _inproj_kernel(x_ref, w_ref, ca_ref, sa_ref, cb_ref, sb_ref, qn_ref, kn_ref, bd_ref,
                   qa_ref, ka_ref, va_ref, qa4_ref, ka4_ref, va4_ref, qa16_ref, ka16_ref, va16_ref,
                   ga_ref, qbt_ref, gb_ref, kbe_ref, kbo_ref, vbt_ref, slab_ref, slab4_ref):
    tm = x_ref.shape[0]
    xb = x_ref[...].astype(BF16)

    def proj(off, width):
        return _dot(xb, w_ref[:, off:off + width])

    lane = lax.broadcasted_iota(jnp.int32, (tm, LANES), 1)
    first_a = (lane % HEAD_DIM) < (HEAD_DIM // 2)
    first_b = (lane % (HEAD_DIM // 2)) < (HEAD_DIM // 4)
    ca, sa = ca_ref[...], sa_ref[...]
    cb, sb = cb_ref[...], sb_ref[...]
    bd = bd_ref[...]

    def rope_a(x):
        return x * ca + _rotate_half(x, first_a, HEAD_DIM // 2) * sa

    def norm_rope_b(x, gain):
        ms = _split_dot(x * x, bd) * (1.0 / HEAD_DIM)
        xn = x * lax.rsqrt(ms + RMS_EPS) * gain
        return xn * cb + _rotate_half(xn, first_b, HEAD_DIM // 4) * sb

    def silu(g):
        return g * (1.0 / (1.0 + jnp.exp(-g)))

    def kv_b(h):
        kb = norm_rope_b(h[:, :KV_WIDTH_B], kn_ref[...])
        kb_swapped = pltpu.roll(kb, HEAD_DIM, 1)
        low = lane < HEAD_DIM
        zero = jnp.zeros_like(kb)
        kbe_ref[0] = jnp.where(low, kb, zero).astype(BF16)
        kbo_ref[0] = jnp.where(low, zero, kb_swapped).astype(BF16)
        kbe_ref[1] = jnp.where(low, kb_swapped, zero).astype(BF16)
        kbo_ref[1] = jnp.where(low, zero, kb).astype(BF16)
        vbt = h[:, KV_WIDTH_B:].T
        for j in range(N_KV_B):
            vbt_ref[j] = vbt[j * HEAD_DIM:(j + 1) * HEAD_DIM, :].astype(BF16)

    def q_b(h):
        qn = qn_ref[...]
        for c in range(WIDTH_B // LANES):
            qb = norm_rope_b(h[:, c * LANES:(c + 1) * LANES], qn) * SCORE_SCALE
            qbt_ref[c] = qb.T.astype(BF16)

    n_slabs = WIDTH_A // LANES
    finish = (lambda x: rope_a(x) * SCORE_SCALE, rope_a, lambda x: x)
    segments = ((OFF_QA, qa_ref, qa4_ref, qa16_ref), (OFF_KA, ka_ref, ka4_ref, ka16_ref),
                (OFF_VA, va_ref, va4_ref, va16_ref))

    def qkv_a(which, h):
        _, nat_ref, ref4, ref16 = segments[which]
        for c in range(n_slabs):
            cols = slice(c * LANES, (c + 1) * LANES)
            val = finish[which](h[:, cols])
            nat_ref[:, cols] = val.astype(BF16)
            slab_ref[which * n_slabs + c] = val
        for c in range(n_slabs):
            cols = slice(c * LANES, (c + 1) * LANES)
            base = (which * n_slabs + c) * 4
            for r4 in range(4):
                part = slab_ref[which * n_slabs + c, pl.ds(r4, tm // 4, stride=4), :]
                ref4[r4, :, cols] = part.astype(BF16)
                slab4_ref[base + r4] = part
            for r16 in range(16):
                part = slab4_ref[base + r16 % 4, pl.ds(r16 // 4, tm // 16, stride=4), :]
                ref16[r16, :, cols] = part.astype(BF16)

    def gate(ref):
        def epilogue(h):
            ref[...] = silu(h).astype(BF16)
        return epilogue

    plan = (
        (OFF_QA, WIDTH_A, functools.partial(qkv_a, 0)),
        (OFF_KB, 2 * KV_WIDTH_B, kv_b),
        (OFF_KA, WIDTH_A, functools.partial(qkv_a, 1)),
        (OFF_QB, WIDTH_B, q_b),
        (OFF_VA, WIDTH_A, functools.partial(qkv_a, 2)),
        (OFF_GA, WIDTH_A, gate(ga_ref)),
        (OFF_GB, WIDTH_B, gate(gb_ref)),
    )
    h = proj(plan[0][0], plan[0][1])
    for i, (_, _, epilogue) in enumerate(plan):
        h_next = proj(plan[i + 1][0], plan[i + 1][1]) if i + 1 < len(plan) else None
        epilogue(h)
        h = h_next


def _rope_tables(n):
    def cos_sin(pos, dim):
        inv_freq = 1.0 / (ROPE_THETA ** (jnp.arange(0, dim, 2, dtype=F32) / dim))
        ang = pos.astype(F32)[:, None] * inv_freq[None, :]
        ang = jnp.concatenate([ang, ang], axis=-1)
        sign = jnp.concatenate([-jnp.ones((dim // 2,), F32), jnp.ones((dim // 2,), F32)])
        return jnp.cos(ang), jnp.sin(ang) * sign[None, :]

    pos = jnp.arange(n, dtype=jnp.int32)
    ca, sa = cos_sin(pos, HEAD_DIM)
    row = pos // GRID_W
    col = pos % GRID_W
    cr, sr = cos_sin(row, HEAD_DIM // 2)
    cc, sc = cos_sin(col, HEAD_DIM // 2)
    cb = jnp.concatenate([cr, cc], axis=-1)
    sb = jnp.concatenate([sr, sc], axis=-1)
    rep = LANES // HEAD_DIM
    return tuple(jnp.tile(t, (1, rep)) for t in (ca, sa, cb, sb))


def _inproj(x, w_bf16, tables, qn, kn, bd):
    b, n, _ = x.shape
    tm = TOKEN_TILE
    nt = n // tm
    t = b * n
    x2 = x.reshape(t, D_MODEL)
    row = lambda i: (i, 0)
    pos = lambda i: (i % nt, 0)
    const = lambda i: (0, 0)
    tab_spec = pl.BlockSpec((tm, LANES), pos)
    wide = lambda w: pl.BlockSpec((tm, w), row)
    def by_residue(dil):
        shape = jax.ShapeDtypeStruct((b, dil, n // dil, WIDTH_A), BF16)
        spec = pl.BlockSpec((None, dil, tm // dil, WIDTH_A), lambda i: (i // nt, 0, i % nt, 0))
        return [shape] * 3, [spec] * 3

    shapes4, specs4 = by_residue(4)
    shapes16, specs16 = by_residue(16)
    out_shapes = (
        jax.ShapeDtypeStruct((t, WIDTH_A), BF16),
        jax.ShapeDtypeStruct((t, WIDTH_A), BF16),
        jax.ShapeDtypeStruct((t, WIDTH_A), BF16),
        *shapes4, *shapes16,
        jax.ShapeDtypeStruct((t, WIDTH_A), BF16),
        jax.ShapeDtypeStruct((b, WIDTH_B // LANES, LANES, n), BF16),
        jax.ShapeDtypeStruct((t, WIDTH_B), BF16),
        jax.ShapeDtypeStruct((b, N_KV_B, n, LANES), BF16),
        jax.ShapeDtypeStruct((b, N_KV_B, n, LANES), BF16),
        jax.ShapeDtypeStruct((b, N_KV_B, HEAD_DIM, n), BF16),
    )
    out_specs = (
        wide(WIDTH_A), wide(WIDTH_A), wide(WIDTH_A), *specs4, *specs16,
        wide(WIDTH_A),
        pl.BlockSpec((None, WIDTH_B // LANES, LANES, tm), lambda i: (i // nt, 0, 0, i % nt)),
        wide(WIDTH_B),
        pl.BlockSpec((None, N_KV_B, tm, LANES), lambda i: (i // nt, 0, i % nt, 0)),
        pl.BlockSpec((None, N_KV_B, tm, LANES), lambda i: (i // nt, 0, i % nt, 0)),
        pl.BlockSpec((None, N_KV_B, HEAD_DIM, tm), lambda i: (i // nt, 0, 0, i % nt)),
    )
    in_specs = [
        pl.BlockSpec((tm, D_MODEL), row),
        pl.BlockSpec((D_MODEL, IN_WIDTH), const),
        tab_spec, tab_spec, tab_spec, tab_spec,
        pl.BlockSpec((1, LANES), const),
        pl.BlockSpec((1, LANES), const),
        pl.BlockSpec((LANES, LANES), const),
    ]
    return pl.pallas_call(
        _inproj_kernel,
        grid=(t // tm,),
        in_specs=in_specs,
        out_specs=out_specs,
        out_shape=out_shapes,
        scratch_shapes=[pltpu.VMEM((3 * WIDTH_A // LANES, tm, LANES), F32),
                        pltpu.VMEM((4 * 3 * WIDTH_A // LANES, tm // 4, LANES), F32)],
        compiler_params=pltpu.CompilerParams(
            dimension_semantics=("arbitrary",), vmem_limit_bytes=VMEM_LIMIT),
        name="inproj",
    )(x2, w_bf16, *tables, qn, kn, bd)


def _gqa_kernel(qt_ref, ke_ref, ko_ref, vt_ref, g_ref, o_ref, s0_ref, s1_ref, m_ref, acc_ref):
    n = ke_ref.shape[0]
    tq, tk = GQA_TQ, GQA_TK
    nc = n // tk
    steps = (n // tq) * nc
    m_ref[...] = jnp.full(m_ref.shape, NEG_BIG, F32)
    acc_ref[...] = jnp.zeros(acc_ref.shape, F32)
    ones = jnp.ones((BF16_ROWS, tk), BF16)

    def offsets(t):
        qi, c = t // nc, t % nc
        if isinstance(t, int):
            return qi * tq, c * tk, c
        return pl.multiple_of(qi * tq, tq), pl.multiple_of(c * tk, tk), c

    def scores(t, s_ref):
        qoff, koff, _ = offsets(t)
        for pr in range(GROUP_B // 2):
            q_pair_t = qt_ref[pr, :, pl.ds(qoff, tq)]
            for odd, k_ref in enumerate((ke_ref, ko_ref)):
                h = 2 * pr + odd
                s_ref[:, h * tq:(h + 1) * tq] = _dot(k_ref[pl.ds(koff, tk), :], q_pair_t)

    def update(t, s_ref):
        _, koff, c = offsets(t)
        s = s_ref[...]
        m_prev = jnp.where(c == 0, NEG_BIG, m_ref[...])
        m_new = jnp.maximum(m_prev, jnp.max(s, axis=0, keepdims=True))
        alpha = jnp.exp2(m_prev - m_new)
        p = jnp.exp2(s - m_new).astype(BF16)
        vt = jnp.concatenate([vt_ref[:, pl.ds(koff, tk)], ones], axis=0)
        acc_ref[...] = alpha * acc_ref[...] + _dot(vt, p)
        m_ref[...] = m_new

    def finalize(t):
        qoff, _, _ = offsets(t)
        acc = acc_ref[...]
        o_t = acc[:HEAD_DIM] * (1.0 / acc[HEAD_DIM:HEAD_DIM + 1])
        for pr in range(GROUP_B // 2):
            pair_t = jnp.concatenate([o_t[:, (2 * pr) * tq:(2 * pr + 1) * tq],
                                      o_t[:, (2 * pr + 1) * tq:(2 * pr + 2) * tq]], axis=0)
            gate = g_ref[pl.ds(qoff, tq), pr * LANES:(pr + 1) * LANES].astype(F32)
            o_ref[pl.ds(qoff, tq), pr * LANES:(pr + 1) * LANES] = (pair_t.T * gate).astype(BF16)

    unroll = min(nc, GQA_MAX_UNROLL)
    bufs = (s0_ref, s1_ref)
    scores(0, bufs[0])

    def body(tt, carry):
        t = unroll * tt
        for u in range(unroll):
            scores(t + u + 1, bufs[(u + 1) % 2])
            update(t + u, bufs[u % 2])

        if unroll == nc:
            finalize(t + unroll - 1)
        else:
            pl.when((t + unroll - 1) % nc == nc - 1)(lambda: finalize(t + unroll - 1))
        return carry

    lax.fori_loop(0, steps // unroll - 1, body, 0)
    for u in range(unroll):
        t = steps - unroll + u
        if u < unroll - 1:
            scores(t + 1, bufs[(u + 1) % 2])
        update(t, bufs[u % 2])
    finalize(steps - 1)


def _gqa(qbt, kbe, kbo, vbt, gb):
    b, n, _ = gb.shape
    tq = GQA_TQ
    gw = GROUP_B * HEAD_DIM
    unroll = min(n // GQA_TK, GQA_MAX_UNROLL)
    assert unroll % 2 == 0 and (n // GQA_TK) % unroll == 0 and n % tq == 0
    rows = pl.BlockSpec((None, n, gw), lambda bi, j: (bi, 0, j))
    keys = pl.BlockSpec((None, None, n, LANES), lambda bi, j: (bi, j, 0, 0))
    return pl.pallas_call(
        _gqa_kernel,
        grid=(b, N_KV_B),
        in_specs=[
            pl.BlockSpec((None, GROUP_B // 2, LANES, n), lambda bi, j: (bi, j, 0, 0)),
            keys, keys,
            pl.BlockSpec((None, None, HEAD_DIM, n), lambda bi, j: (bi, j, 0, 0)),
            rows,
        ],
        out_specs=rows,
        out_shape=jax.ShapeDtypeStruct((b, n, WIDTH_B), BF16),
        scratch_shapes=[
            pltpu.VMEM((GQA_TK, GROUP_B * tq), F32),
            pltpu.VMEM((GQA_TK, GROUP_B * tq), F32),
            pltpu.VMEM((1, GROUP_B * tq), F32),
            pltpu.VMEM((HEAD_DIM + BF16_ROWS, GROUP_B * tq), F32),
        ],
        compiler_params=pltpu.CompilerParams(
            dimension_semantics=("arbitrary", "arbitrary"),
            vmem_limit_bytes=VMEM_LIMIT),
        name="gqa",
    )(qbt, kbe, kbo, vbt, gb)


def _dilated_kernel(q_ref, k_ref, v_ref, bias_ref, o_ref, lse_ref, s0_ref, s1_ref):
    dil, length = q_ref.shape[0], q_ref.shape[1]
    tq = DIL_TQ
    nb = length // tq
    steps = dil * nb
    win = min(tq + 2 * DIL_HALF, length)
    n_pairs = N_HEADS_A // 2
    low_lane = lax.broadcasted_iota(jnp.int32, (tq, LANES), 1) < HEAD_DIM
    low_row = lax.broadcasted_iota(jnp.int32, (LANES, tq), 0) < HEAD_DIM
    row8 = lax.broadcasted_iota(jnp.int32, (8, tq), 0)

    def window(t):
        r, i = t // nb, t % nb
        if isinstance(t, int):
            a0 = i * tq
            ws = min(max(a0 - DIL_HALF, 0), length - win)
            table = 0 if i == 0 else (2 if i == nb - 1 else 1)
        else:
            a0 = pl.multiple_of(i * tq, tq)
            ws = pl.multiple_of(jnp.clip(a0 - DIL_HALF, 0, length - win), DIL_HALF)
            table = jnp.where(i == 0, 0, jnp.where(i == nb - 1, 2, 1))
        return r, table, a0, ws

    def scores(t, s_ref):
        r, table, a0, ws = window(t)
        bias = bias_ref[table]
        for pair in range(n_pairs):
            cols = slice(pair * LANES, (pair + 1) * LANES)
            q2 = q_ref[r, pl.ds(a0, tq), cols]
            k2 = k_ref[r, pl.ds(ws, win), cols]
            zero = jnp.zeros_like(q2)
            qs = jnp.concatenate([jnp.where(low_lane, q2, zero),
                                  jnp.where(low_lane, zero, q2)], axis=0)
            s_ref[pair] = _dot_nt(k2, qs) + bias

    def attend(t, s_ref):
        r, _, a0, ws = window(t)
        lse8 = jnp.zeros((8, tq), F32)
        for pair in range(n_pairs):
            cols = slice(pair * LANES, (pair + 1) * LANES)
            v2 = v_ref[r, pl.ds(ws, win), cols]
            s = s_ref[pair]
            m = jnp.max(s, axis=0, keepdims=True)
            p = jnp.exp2(s - m)
            den = jnp.sum(p, axis=0, keepdims=True)
            o_t = lax.dot_general(v2, p.astype(BF16), (((0,), (0,)), ((), ())),
                                  preferred_element_type=F32)
            o_t = o_t * (1.0 / den)
            pair_t = jnp.where(low_row, o_t[:, :tq], o_t[:, tq:])
            o_ref[r, pl.ds(a0, tq), cols] = pair_t.T.astype(BF16)
            lse = (m + jnp.log2(den)) * LN2
            lse8 = jnp.where(row8 == 2 * pair, lse[:, :tq], lse8)
            lse8 = jnp.where(row8 == 2 * pair + 1, lse[:, tq:], lse8)
        lse_t = jnp.concatenate([lse8, jnp.zeros((LANES - 8, tq), F32)], axis=0)
        lse_ref[r, pl.ds(a0, tq), :] = lse_t.T

    scores(0, s0_ref)

    def body(tt, carry):
        t = 2 * tt
        scores(t + 1, s1_ref)
        attend(t, s0_ref)
        scores(t + 2, s0_ref)
        attend(t + 1, s1_ref)
        return carry

    lax.fori_loop(0, steps // 2 - 1, body, 0)
    scores(steps - 1, s1_ref)
    attend(steps - 2, s0_ref)
    attend(steps - 1, s1_ref)


def _dilated(qa, ka, va):
    b, dil, length, _ = qa.shape
    assert (dil * (length // DIL_TQ)) % 2 == 0
    win = min(DIL_TQ + 2 * DIL_HALF, length)
    s_scratch = pltpu.VMEM((N_HEADS_A // 2, win, 2 * DIL_TQ), F32)
    rel = np.arange(win)[:, None] - (np.arange(2 * DIL_TQ) % DIL_TQ)[None, :]
    bias = np.stack([np.where(np.abs(rel + off) <= DIL_HALF, 0.0, NEG_BIG)
                     for off in (0, -DIL_HALF, DIL_TQ - win)]).astype(np.float32)
    bias_spec = pl.BlockSpec(bias.shape, lambda bi: (0, 0, 0))
    spec = pl.BlockSpec((None, dil, length, WIDTH_A), lambda bi: (bi, 0, 0, 0))
    lse_spec = pl.BlockSpec((None, dil, length, LANES), lambda bi: (bi, 0, 0, 0))
    return pl.pallas_call(
        _dilated_kernel,
        grid=(b,),
        in_specs=[spec, spec, spec, bias_spec],
        out_specs=(spec, lse_spec),
        out_shape=(jax.ShapeDtypeStruct((b, dil, length, WIDTH_A), BF16),
                   jax.ShapeDtypeStruct((b, dil, length, LANES), F32)),
        scratch_shapes=[s_scratch, s_scratch],
        compiler_params=pltpu.CompilerParams(
            dimension_semantics=("arbitrary",), vmem_limit_bytes=VMEM_LIMIT),
        name=f"dilated{dil}",
    )(qa, ka, va, jnp.asarray(bias))


def _outproj_kernel(x_ref, o1_ref, o2_ref, o3_ref, l1_ref, l2_ref, l3_ref, ga_ref, mb_ref,
                    w_ref, ex_ref, lg_ref, lb_ref, y_ref, o_scr, l_scr, mix_scr):
    tm = x_ref.shape[0]
    n_slabs = WIDTH_A // LANES
    rc = OUT_ROW_CHUNK

    for slot, (o_ref, l_ref) in enumerate(((o2_ref, l2_ref), (o3_ref, l3_ref))):
        dil = o_ref.shape[0]
        for r in range(dil):
            for c in range(n_slabs):
                part = o_ref[r, :, c * LANES:(c + 1) * LANES].astype(F32)
                o_scr[slot * n_slabs + c, pl.ds(r, tm // dil, stride=dil), :] = part
            l_scr[slot, pl.ds(r, tm // dil, stride=dil), :] = l_ref[r]

    ex = ex_ref[...]

    def mix(rows):
        lses = (l1_ref[0, rows, :], l_scr[0, rows, :], l_scr[1, rows, :])
        mx = jnp.maximum(jnp.maximum(lses[0], lses[1]), lses[2])
        es = [jnp.exp(l - mx) for l in lses]
        inv = 1.0 / (es[0] + es[1] + es[2])
        o1 = o1_ref[0, rows, :].astype(F32)
        out_a = o1
        for slot in range(2):
            w = _dot((es[slot + 1] * inv).astype(BF16), ex)
            o = jnp.concatenate([o_scr[slot * n_slabs + c, rows, :] for c in range(n_slabs)], axis=1)
            out_a = out_a + w * (o - o1)
        return (out_a * ga_ref[rows, :].astype(F32)).astype(BF16)

    def norm(rows, f):
        z = DEEPNORM_ALPHA * x_ref[rows, :] + f
        mu = jnp.mean(z, axis=-1, keepdims=True)
        zc = z - mu
        var = jnp.mean(zc * zc, axis=-1, keepdims=True)
        y_ref[rows, :] = zc * lax.rsqrt(var + LN_EPS) * lg_ref[...] + lb_ref[...]

    chunks = [slice(ch * rc, (ch + 1) * rc) for ch in range(tm // rc)]
    for rows in chunks:
        mix_scr[rows, :] = mix(rows)
    f = _dot(mix_scr[...], w_ref[:WIDTH_A, :]) + _dot(mb_ref[...], w_ref[WIDTH_A:, :])
    for rows in chunks:
        norm(rows, f[rows, :])


def _outproj(x2, o_list, lse_list, ga, mix_b, w_out_bf16, expand, ln_g, ln_b):
    t = x2.shape[0]
    tm = TOKEN_TILE
    nt = o_list[0].shape[1] * o_list[0].shape[2] // tm
    row = lambda i: (i, 0)
    const = lambda i: (0, 0)
    wide = lambda w: pl.BlockSpec((tm, w), row)

    def by_residue(a):
        _, dil, _, width = a.shape
        return pl.BlockSpec((None, dil, tm // dil, width), lambda i: (i // nt, 0, i % nt, 0))

    n_strided = sum(o.shape[1] > 1 for o in o_list)
    return pl.pallas_call(
        _outproj_kernel,
        grid=(t // tm,),
        in_specs=[
            wide(D_MODEL),
            *[by_residue(o) for o in o_list],
            *[by_residue(l) for l in lse_list],
            wide(WIDTH_A), wide(WIDTH_B),
            pl.BlockSpec((WIDTH_A + WIDTH_B, D_MODEL), const),
            pl.BlockSpec((LANES, WIDTH_A), const),
            pl.BlockSpec((1, D_MODEL), const),
            pl.BlockSpec((1, D_MODEL), const),
        ],
        out_specs=wide(D_MODEL),
        out_shape=jax.ShapeDtypeStruct((t, D_MODEL), F32),
        scratch_shapes=[pltpu.VMEM((n_strided * WIDTH_A // LANES, tm, LANES), F32),
                        pltpu.VMEM((n_strided, tm, LANES), F32),
                        pltpu.VMEM((tm, WIDTH_A), BF16)],
        compiler_params=pltpu.CompilerParams(
            dimension_semantics=("arbitrary",), vmem_limit_bytes=VMEM_LIMIT),
        name="outproj",
    )(x2, *o_list, *lse_list, ga, mix_b, w_out_bf16, expand, ln_g, ln_b)


def _constants():
    head_of_lane = np.arange(LANES) // HEAD_DIM
    bd = (head_of_lane[:, None] == head_of_lane[None, :]).astype(np.float32)
    ex = (np.arange(LANES)[:, None] == (np.arange(WIDTH_A) // HEAD_DIM)[None, :]).astype(np.float32)
    return jnp.asarray(bd, BF16), jnp.asarray(ex, BF16)


def _layer(x, w_in_bf16, w_out_bf16, tables, qn, kn, ln_g, ln_b, bd, expand):
    b, n, _ = x.shape
    (qa, ka, va, qa4, ka4, va4, qa16, ka16, va16,
     ga, qbt, gb, kbe, kbo, vbt) = _inproj(x, w_in_bf16, tables, qn, kn, bd)
    mix_b = _gqa(qbt, kbe, kbo, vbt, gb.reshape(b, n, WIDTH_B))
    assert tuple(d for _, d in DILATED_PATTERNS) == (1, 4, 16)
    one = lambda a: a.reshape(b, 1, n, WIDTH_A)
    o_list, lse_list = [], []
    for qkv in ((one(qa), one(ka), one(va)), (qa4, ka4, va4), (qa16, ka16, va16)):
        o, lse = _dilated(*qkv)
        o_list.append(o)
        lse_list.append(lse)
    y = _outproj(x.reshape(b * n, D_MODEL), o_list, lse_list, ga, mix_b.reshape(b * n, WIDTH_B),
                 w_out_bf16, expand, ln_g, ln_b)
    return y.reshape(b, n, D_MODEL)


def kernel(x_prompt, x_sample, w_in, w_out, q_norm, k_norm, ln_g, ln_b):
    bd, expand = _constants()
    tables = _rope_tables(max(x_prompt.shape[1], x_sample.shape[1]))
    rep = LANES // HEAD_DIM
    y_prompt, y_sample = x_prompt, x_sample
    for i in range(w_in.shape[0]):
        w_in_bf16 = w_in[i].astype(BF16)
        w_out_bf16 = w_out[i].astype(BF16)
        qn = jnp.tile(q_norm[i].reshape(1, HEAD_DIM), (1, rep))
        kn = jnp.tile(k_norm[i].reshape(1, HEAD_DIM), (1, rep))
        args = (w_in_bf16, w_out_bf16, tables, qn, kn,
                ln_g[i].reshape(1, D_MODEL), ln_b[i].reshape(1, D_MODEL), bd, expand)
        y_prompt = _layer(y_prompt, *args)
        y_sample = _layer(y_sample, *args)
    return (y_prompt, y_sample)
```

```python
import functools
import math

import numpy as np
import jax
import jax.numpy as jnp
from jax import lax
from jax.experimental import pallas as pl
from jax.experimental.pallas import tpu as pltpu

D_MODEL = 1024
HEAD_DIM = 64
N_HEADS_A = 8
N_HEADS_B = 8
N_KV_B = 2
GROUP_B = N_HEADS_B // N_KV_B
WIDTH_A = N_HEADS_A * HEAD_DIM
WIDTH_B = N_HEADS_B * HEAD_DIM
KV_WIDTH_B = N_KV_B * HEAD_DIM
DILATED_PATTERNS = ((128, 1), (512, 4), (2048, 16))
GRID_W = 64
ROPE_THETA = 10000.0
LN_EPS = 1e-5
RMS_EPS = 1e-6
NEG_BIG = -1e30
DEPTH = 1
DEEPNORM_ALPHA = (2.0 * DEPTH) ** 0.25

OFF_QA = 0
OFF_KA = OFF_QA + WIDTH_A
OFF_VA = OFF_KA + WIDTH_A
OFF_GA = OFF_VA + WIDTH_A
OFF_QB = OFF_GA + WIDTH_A
OFF_KB = OFF_QB + WIDTH_B
OFF_VB = OFF_KB + KV_WIDTH_B
OFF_GB = OFF_VB + KV_WIDTH_B
IN_WIDTH = OFF_GB + WIDTH_B

LANES = 128
LOG2E = 1.4426950408889634
LN2 = 0.6931471805599453
SCORE_SCALE = (HEAD_DIM ** -0.5) * LOG2E

TOKEN_TILE = 512
GQA_TQ = 256
GQA_TK = 512
GQA_MAX_UNROLL = 8
BF16_ROWS = 16
OUT_ROW_CHUNK = 128
DIL_TQ = 128
DIL_HALF = 64
VMEM_LIMIT = 48 * 1024 * 1024

F32 = jnp.float32
BF16 = jnp.bfloat16


def _dot(a, b):
    return jnp.dot(a, b, preferred_element_type=F32)


def _dot_nt(a, b):
    return lax.dot_general(a, b, (((1,), (1,)), ((), ())), preferred_element_type=F32)


def _split_dot(x, w_bf16):
    hi = x.astype(BF16)
    lo = (x - hi.astype(F32)).astype(BF16)
    return _dot(hi, w_bf16) + _dot(lo, w_bf16)


def _rotate_half(x, first, half):
    return jnp.where(first, pltpu.roll(x, LANES - half, 1), pltpu.roll(x, half, 1))


def _inproj_kernel(x_ref, w_ref, ca_ref, sa_ref, cb_ref, sb_ref, qn_ref, kn_ref, bd_ref,
                   qa_ref, ka_ref, va_ref, qa4_ref, ka4_ref, va4_ref, qa16_ref, ka16_ref, va16_ref,
                   ga_ref, qbt_ref, gb_ref, kbe_ref, kbo_ref, vbt_ref, slab_ref, slab4_ref):
    tm = x_ref.shape[0]
    xb = x_ref[...].astype(BF16)

    def proj(off, width):
        return _dot(xb, w_ref[:, off:off + width])

    lane = lax.broadcasted_iota(jnp.int32, (tm, LANES), 1)
    first_a = (lane % HEAD_DIM) < (HEAD_DIM // 2)
    first_b = (lane % (HEAD_DIM // 2)) < (HEAD_DIM // 4)
    ca, sa = ca_ref[...], sa_ref[...]
    cb, sb = cb_ref[...], sb_ref[...]
    bd = bd_ref[...]

    def rope_a(x):
        return x * ca + _rotate_half(x, first_a, HEAD_DIM // 2) * sa

    def norm_rope_b(x, gain):
        ms = _split_dot(x * x, bd) * (1.0 / HEAD_DIM)
        xn = x * lax.rsqrt(ms + RMS_EPS) * gain
        return xn * cb + _rotate_half(xn, first_b, HEAD_DIM // 4) * sb

    def silu(g):
        return g * (1.0 / (1.0 + jnp.exp(-g)))

    def kv_b(h):
        kb = norm_rope_b(h[:, :KV_WIDTH_B], kn_ref[...])
        kb_swapped = pltpu.roll(kb, HEAD_DIM, 1)
        low = lane < HEAD_DIM
        zero = jnp.zeros_like(kb)
        kbe_ref[0] = jnp.where(low, kb, zero).astype(BF16)
        kbo_ref[0] = jnp.where(low, zero, kb_swapped).astype(BF16)
        kbe_ref[1] = jnp.where(low, kb_swapped, zero).astype(BF16)
        kbo_ref[1] = jnp.where(low, zero, kb).astype(BF16)
        vbt = h[:, KV_WIDTH_B:].T
        for j in range(N_KV_B):
            vbt_ref[j] = vbt[j * HEAD_DIM:(j + 1) * HEAD_DIM, :].astype(BF16)

    def q_b(h):
        qn = qn_ref[...]
        for c in range(WIDTH_B // LANES):
            qb = norm_rope_b(h[:, c * LANES:(c + 1) * LANES], qn) * SCORE_SCALE
            qbt_ref[c] = qb.T.astype(BF16)

    n_slabs = WIDTH_A // LANES
    finish = (lambda x: rope_a(x) * SCORE_SCALE, rope_a, lambda x: x)
    segments = ((OFF_QA, qa_ref, qa4_ref, qa16_ref), (OFF_KA, ka_ref, ka4_ref, ka16_ref),
                (OFF_VA, va_ref, va4_ref, va16_ref))

    def qkv_a(which, h):
        _, nat_ref, ref4, ref16 = segments[which]
        for c in range(n_slabs):
            cols = slice(c * LANES, (c + 1) * LANES)
            val = finish[which](h[:, cols])
            nat_ref[:, cols] = val.astype(BF16)
            slab_ref[which * n_slabs + c] = val
        for c in range(n_slabs):
            cols = slice(c * LANES, (c + 1) * LANES)
            base = (which * n_slabs + c) * 4
            for r4 in range(4):
                part = slab_ref[which * n_slabs + c, pl.ds(r4, tm // 4, stride=4), :]
                ref4[r4, :, cols] = part.astype(BF16)
                slab4_ref[base + r4] = part
            for r16 in range(16):
                part = slab4_ref[base + r16 % 4, pl.ds(r16 // 4, tm // 16, stride=4), :]
                ref16[r16, :, cols] = part.astype(BF16)

    def gate(ref):
        def epilogue(h):
            ref[...] = silu(h).astype(BF16)
        return epilogue

    plan = (
        (OFF_QA, WIDTH_A, functools.partial(qkv_a, 0)),
        (OFF_KB, 2 * KV_WIDTH_B, kv_b),
        (OFF_KA, WIDTH_A, functools.partial(qkv_a, 1)),
        (OFF_QB, WIDTH_B, q_b),
        (OFF_VA, WIDTH_A, functools.partial(qkv_a, 2)),
        (OFF_GA, WIDTH_A, gate(ga_ref)),
        (OFF_GB, WIDTH_B, gate(gb_ref)),
    )
    h = proj(plan[0][0], plan[0][1])
    for i, (_, _, epilogue) in enumerate(plan):
        h_next = proj(plan[i + 1][0], plan[i + 1][1]) if i + 1 < len(plan) else None
        epilogue(h)
        h = h_next


def _rope_tables(n):
    def cos_sin(pos, dim):
        inv_freq = 1.0 / (ROPE_THETA ** (jnp.arange(0, dim, 2, dtype=F32) / dim))
        ang = pos.astype(F32)[:, None] * inv_freq[None, :]
        ang = jnp.concatenate([ang, ang], axis=-1)
        sign = jnp.concatenate([-jnp.ones((dim // 2,), F32), jnp.ones((dim // 2,), F32)])
        return jnp.cos(ang), jnp.sin(ang) * sign[None, :]

    pos = jnp.arange(n, dtype=jnp.int32)
    ca, sa = cos_sin(pos, HEAD_DIM)
    row = pos // GRID_W
    col = pos % GRID_W
    cr, sr = cos_sin(row, HEAD_DIM // 2)
    cc, sc = cos_sin(col, HEAD_DIM // 2)
    cb = jnp.concatenate([cr, cc], axis=-1)
    sb = jnp.concatenate([sr, sc], axis=-1)
    rep = LANES // HEAD_DIM
    return tuple(jnp.tile(t, (1, rep)) for t in (ca, sa, cb, sb))


def _inproj(x, w_bf16, tables, qn, kn, bd):
    b, n, _ = x.shape
    tm = TOKEN_TILE
    nt = n // tm
    t = b * n
    x2 = x.reshape(t, D_MODEL)
    row = lambda i: (i, 0)
    pos = lambda i: (i % nt, 0)
    const = lambda i: (0, 0)
    tab_spec = pl.BlockSpec((tm, LANES), pos)
    wide = lambda w: pl.BlockSpec((tm, w), row)
    def by_residue(dil):
        shape = jax.ShapeDtypeStruct((b, dil, n // dil, WIDTH_A), BF16)
        spec = pl.BlockSpec((None, dil, tm // dil, WIDTH_A), lambda i: (i // nt, 0, i % nt, 0))
        return [shape] * 3, [spec] * 3

    shapes4, specs4 = by_residue(4)
    shapes16, specs16 = by_residue(16)
    out_shapes = (
        jax.ShapeDtypeStruct((t, WIDTH_A), BF16),
        jax.ShapeDtypeStruct((t, WIDTH_A), BF16),
        jax.ShapeDtypeStruct((t, WIDTH_A), BF16),
        *shapes4, *shapes16,
        jax.ShapeDtypeStruct((t, WIDTH_A), BF16),
        jax.ShapeDtypeStruct((b, WIDTH_B // LANES, LANES, n), BF16),
        jax.ShapeDtypeStruct((t, WIDTH_B), BF16),
        jax.ShapeDtypeStruct((b, N_KV_B, n, LANES), BF16),
        jax.ShapeDtypeStruct((b, N_KV_B, n, LANES), BF16),
        jax.ShapeDtypeStruct((b, N_KV_B, HEAD_DIM, n), BF16),
    )
    out_specs = (
        wide(WIDTH_A), wide(WIDTH_A), wide(WIDTH_A), *specs4, *specs16,
        wide(WIDTH_A),
        pl.BlockSpec((None, WIDTH_B // LANES, LANES, tm), lambda i: (i // nt, 0, 0, i % nt)),
        wide(WIDTH_B),
        pl.BlockSpec((None, N_KV_B, tm, LANES), lambda i: (i // nt, 0, i % nt, 0)),
        pl.BlockSpec((None, N_KV_B, tm, LANES), lambda i: (i // nt, 0, i % nt, 0)),
        pl.BlockSpec((None, N_KV_B, HEAD_DIM, tm), lambda i: (i // nt, 0, 0, i % nt)),
    )
    in_specs = [
        pl.BlockSpec((tm, D_MODEL), row),
        pl.BlockSpec((D_MODEL, IN_WIDTH), const),
        tab_spec, tab_spec, tab_spec, tab_spec,
        pl.BlockSpec((1, LANES), const),
        pl.BlockSpec((1, LANES), const),
        pl.BlockSpec((LANES, LANES), const),
    ]
    return pl.pallas_call(
        _inproj_kernel,
        grid=(t // tm,),
        in_specs=in_specs,
        out_specs=out_specs,
        out_shape=out_shapes,
        scratch_shapes=[pltpu.VMEM((3 * WIDTH_A // LANES, tm, LANES), F32),
                        pltpu.VMEM((4 * 3 * WIDTH_A // LANES, tm // 4, LANES), F32)],
        compiler_params=pltpu.CompilerParams(
            dimension_semantics=("arbitrary",), vmem_limit_bytes=VMEM_LIMIT),
        name="inproj",
    )(x2, w_bf16, *tables, qn, kn, bd)


def _gqa_kernel(qt_ref, ke_ref, ko_ref, vt_ref, g_ref, o_ref, s0_ref, s1_ref, m_ref, acc_ref):
    n = ke_ref.shape[0]
    tq, tk = GQA_TQ, GQA_TK
    nc = n // tk
    steps = (n // tq) * nc
    m_ref[...] = jnp.full(m_ref.shape, NEG_BIG, F32)
    acc_ref[...] = jnp.zeros(acc_ref.shape, F32)
    ones = jnp.ones((BF16_ROWS, tk), BF16)

    def offsets(t):
        qi, c = t // nc, t % nc
        if isinstance(t, int):
            return qi * tq, c * tk, c
        return pl.multiple_of(qi * tq, tq), pl.multiple_of(c * tk, tk), c

    def scores(t, s_ref):
        qoff, koff, _ = offsets(t)
        for pr in range(GROUP_B // 2):
            q_pair_t = qt_ref[pr, :, pl.ds(qoff, tq)]
            for odd, k_ref in enumerate((ke_ref, ko_ref)):
                h = 2 * pr + odd
                s_ref[:, h * tq:(h + 1) * tq] = _dot(k_ref[pl.ds(koff, tk), :], q_pair_t)

    def update(t, s_ref):
        _, koff, c = offsets(t)
        m_prev = jnp.where(c == 0, NEG_BIG, m_ref[...])
        m_new = jnp.maximum(m_prev, jnp.max(s_ref[...], axis=0, keepdims=True))
        alpha = jnp.exp2(m_prev - m_new)
        again = 0 if isinstance(c, int) else pl.multiple_of(jnp.minimum(c, 0) * tk, tk)
        p = jnp.exp2(s_ref[pl.ds(again, tk), :] - m_new).astype(BF16)
        vt = jnp.concatenate([vt_ref[:, pl.ds(koff, tk)], ones], axis=0)
        acc_ref[...] = alpha * acc_ref[...] + _dot(vt, p)
        m_ref[...] = m_new

    def finalize(t):
        qoff, _, _ = offsets(t)
        acc = acc_ref[...]
        o_t = acc[:HEAD_DIM] * (1.0 / acc[HEAD_DIM:HEAD_DIM + 1])
        for pr in range(GROUP_B // 2):
            pair_t = jnp.concatenate([o_t[:, (2 * pr) * tq:(2 * pr + 1) * tq],
                                      o_t[:, (2 * pr + 1) * tq:(2 * pr + 2) * tq]], axis=0)
            gate = g_ref[pl.ds(qoff, tq), pr * LANES:(pr + 1) * LANES].astype(F32)
            o_ref[pl.ds(qoff, tq), pr * LANES:(pr + 1) * LANES] = (pair_t.T * gate).astype(BF16)

    unroll = min(nc, GQA_MAX_UNROLL)
    bufs = (s0_ref, s1_ref)
    scores(0, bufs[0])

    def body(tt, carry):
        t = unroll * tt
        for u in range(unroll):
            scores(t + u + 1, bufs[(u + 1) % 2])
            update(t + u, bufs[u % 2])

        if unroll == nc:
            finalize(t + unroll - 1)
        else:
            pl.when((t + unroll - 1) % nc == nc - 1)(lambda: finalize(t + unroll - 1))
        return carry

    lax.fori_loop(0, steps // unroll - 1, body, 0)
    for u in range(unroll):
        t = steps - unroll + u
        if u < unroll - 1:
            scores(t + 1, bufs[(u + 1) % 2])
        update(t, bufs[u % 2])
    finalize(steps - 1)


def _gqa(qbt, kbe, kbo, vbt, gb):
    b, n, _ = gb.shape
    tq = GQA_TQ
    gw = GROUP_B * HEAD_DIM
    unroll = min(n // GQA_TK, GQA_MAX_UNROLL)
    assert unroll % 2 == 0 and (n // GQA_TK) % unroll == 0 and n % tq == 0
    rows = pl.BlockSpec((None, n, gw), lambda bi, j: (bi, 0, j))
    keys = pl.BlockSpec((None, None, n, LANES), lambda bi, j: (bi, j, 0, 0))
    return pl.pallas_call(
        _gqa_kernel,
        grid=(b, N_KV_B),
        in_specs=[
            pl.BlockSpec((None, GROUP_B // 2, LANES, n), lambda bi, j: (bi, j, 0, 0)),
            keys, keys,
            pl.BlockSpec((None, None, HEAD_DIM, n), lambda bi, j: (bi, j, 0, 0)),
            rows,
        ],
        out_specs=rows,
        out_shape=jax.ShapeDtypeStruct((b, n, WIDTH_B), BF16),
        scratch_shapes=[
            pltpu.VMEM((GQA_TK, GROUP_B * tq), F32),
            pltpu.VMEM((GQA_TK, GROUP_B * tq), F32),
            pltpu.VMEM((1, GROUP_B * tq), F32),
            pltpu.VMEM((HEAD_DIM + BF16_ROWS, GROUP_B * tq), F32),
        ],
        compiler_params=pltpu.CompilerParams(
            dimension_semantics=("arbitrary", "arbitrary"),
            vmem_limit_bytes=VMEM_LIMIT),
        name="gqa",
    )(qbt, kbe, kbo, vbt, gb)


def _dilated_kernel(q_ref, k_ref, v_ref, bias_ref, o_ref, lse_ref, s0_ref, s1_ref):
    dil, length = q_ref.shape[0], q_ref.shape[1]
    tq = DIL_TQ
    nb = length // tq
    steps = dil * nb
    win = min(tq + 2 * DIL_HALF, length)
    n_pairs = N_HEADS_A // 2
    low_lane = lax.broadcasted_iota(jnp.int32, (tq, LANES), 1) < HEAD_DIM
    low_row = lax.broadcasted_iota(jnp.int32, (LANES, tq), 0) < HEAD_DIM
    row8 = lax.broadcasted_iota(jnp.int32, (8, tq), 0)

    def window(t):
        r, i = t // nb, t % nb
        if isinstance(t, int):
            a0 = i * tq
            ws = min(max(a0 - DIL_HALF, 0), length - win)
            table = 0 if i == 0 else (2 if i == nb - 1 else 1)
        else:
            a0 = pl.multiple_of(i * tq, tq)
            ws = pl.multiple_of(jnp.clip(a0 - DIL_HALF, 0, length - win), DIL_HALF)
            table = jnp.where(i == 0, 0, jnp.where(i == nb - 1, 2, 1))
        return r, table, a0, ws

    def scores(t, s_ref):
        r, table, a0, ws = window(t)
        bias = bias_ref[table]
        for pair in range(n_pairs):
            cols = slice(pair * LANES, (pair + 1) * LANES)
            q2 = q_ref[r, pl.ds(a0, tq), cols]
            k2 = k_ref[r, pl.ds(ws, win), cols]
            q2_t = q2.T
            zero = jnp.zeros_like(q2_t)
            qs_t = jnp.concatenate([jnp.where(low_row, q2_t, zero),
                                    jnp.where(low_row, zero, q2_t)], axis=1)
            s_ref[pair] = _dot(k2, qs_t) + bias

    def attend(t, s_ref):
        r, _, a0, ws = window(t)
        lse8 = jnp.zeros((8, tq), F32)
        for pair in range(n_pairs):
            cols = slice(pair * LANES, (pair + 1) * LANES)
            v2 = v_ref[r, pl.ds(ws, win), cols]
            m = jnp.max(s_ref[pair], axis=0, keepdims=True)
            again = 0 if isinstance(t, int) else pl.multiple_of(jnp.minimum(t, 0) * win, win)
            p = jnp.exp2(s_ref[pair, pl.ds(again, win), :] - m)
            den = jnp.sum(p, axis=0, keepdims=True)
            o_t = lax.dot_general(v2, p.astype(BF16), (((0,), (0,)), ((), ())),
                                  preferred_element_type=F32)
            o_t = o_t * (1.0 / den)
            pair_t = jnp.where(low_row, o_t[:, :tq], o_t[:, tq:])
            o_ref[r, pl.ds(a0, tq), cols] = pair_t.T.astype(BF16)
            lse = (m + jnp.log2(den)) * LN2
            lse8 = jnp.where(row8 == 2 * pair, lse[:, :tq], lse8)
            lse8 = jnp.where(row8 == 2 * pair + 1, lse[:, tq:], lse8)
        lse_t = jnp.concatenate([lse8, jnp.zeros((LANES - 8, tq), F32)], axis=0)
        lse_ref[r, pl.ds(a0, tq), :] = lse_t.T

    scores(0, s0_ref)

    def body(tt, carry):
        t = 2 * tt
        scores(t + 1, s1_ref)
        attend(t, s0_ref)
        scores(t + 2, s0_ref)
        attend(t + 1, s1_ref)
        return carry

    lax.fori_loop(0, steps // 2 - 1, body, 0)
    scores(steps - 1, s1_ref)
    attend(steps - 2, s0_ref)
    attend(steps - 1, s1_ref)


def _dilated(qa, ka, va):
    b, dil, length, _ = qa.shape
    assert (dil * (length // DIL_TQ)) % 2 == 0
    win = min(DIL_TQ + 2 * DIL_HALF, length)
    s_scratch = pltpu.VMEM((N_HEADS_A // 2, win, 2 * DIL_TQ), F32)
    rel = np.arange(win)[:, None] - (np.arange(2 * DIL_TQ) % DIL_TQ)[None, :]
    bias = np.stack([np.where(np.abs(rel + off) <= DIL_HALF, 0.0, NEG_BIG)
                     for off in (0, -DIL_HALF, DIL_TQ - win)]).astype(np.float32)
    bias_spec = pl.BlockSpec(bias.shape, lambda bi: (0, 0, 0))
    spec = pl.BlockSpec((None, dil, length, WIDTH_A), lambda bi: (bi, 0, 0, 0))
    lse_spec = pl.BlockSpec((None, dil, length, LANES), lambda bi: (bi, 0, 0, 0))
    return pl.pallas_call(
        _dilated_kernel,
        grid=(b,),
        in_specs=[spec, spec, spec, bias_spec],
        out_specs=(spec, lse_spec),
        out_shape=(jax.ShapeDtypeStruct((b, dil, length, WIDTH_A), BF16),
                   jax.ShapeDtypeStruct((b, dil, length, LANES), F32)),
        scratch_shapes=[s_scratch, s_scratch],
        compiler_params=pltpu.CompilerParams(
            dimension_semantics=("arbitrary",), vmem_limit_bytes=VMEM_LIMIT),
        name=f"dilated{dil}",
    )(qa, ka, va, jnp.asarray(bias))


def _outproj_kernel(x_ref, o1_ref, o2_ref, o3_ref, l1_ref, l2_ref, l3_ref, ga_ref, mb_ref,
                    w_ref, ex_ref, lg_ref, lb_ref, y_ref, o_scr, l_scr, mix_scr):
    tm = x_ref.shape[0]
    n_slabs = WIDTH_A // LANES
    rc = OUT_ROW_CHUNK

    for slot, (o_ref, l_ref) in enumerate(((o2_ref, l2_ref), (o3_ref, l3_ref))):
        dil = o_ref.shape[0]
        for r in range(dil):
            for c in range(n_slabs):
                part = o_ref[r, :, c * LANES:(c + 1) * LANES].astype(F32)
                o_scr[slot * n_slabs + c, pl.ds(r, tm // dil, stride=dil), :] = part
            l_scr[slot, pl.ds(r, tm // dil, stride=dil), :] = l_ref[r]

    ex = ex_ref[...]

    def mix(rows):
        lses = (l1_ref[0, rows, :], l_scr[0, rows, :], l_scr[1, rows, :])
        mx = jnp.maximum(jnp.maximum(lses[0], lses[1]), lses[2])
        es = [jnp.exp(l - mx) for l in lses]
        inv = 1.0 / (es[0] + es[1] + es[2])
        o1 = o1_ref[0, rows, :].astype(F32)
        out_a = o1
        for slot in range(2):
            w = _dot((es[slot + 1] * inv).astype(BF16), ex)
            o = jnp.concatenate([o_scr[slot * n_slabs + c, rows, :] for c in range(n_slabs)], axis=1)
            out_a = out_a + w * (o - o1)
        return (out_a * ga_ref[rows, :].astype(F32)).astype(BF16)

    def norm(rows, f):
        z = DEEPNORM_ALPHA * x_ref[rows, :] + f
        mu = jnp.mean(z, axis=-1, keepdims=True)
        zc = z - mu
        var = jnp.mean(zc * zc, axis=-1, keepdims=True)
        y_ref[rows, :] = zc * lax.rsqrt(var + LN_EPS) * lg_ref[...] + lb_ref[...]

    chunks = [slice(ch * rc, (ch + 1) * rc) for ch in range(tm // rc)]
    for rows in chunks:
        mix_scr[rows, :] = mix(rows)
    f = _dot(mix_scr[...], w_ref[:WIDTH_A, :]) + _dot(mb_ref[...], w_ref[WIDTH_A:, :])
    for rows in chunks:
        norm(rows, f[rows, :])


def _outproj(x2, o_list, lse_list, ga, mix_b, w_out_bf16, expand, ln_g, ln_b):
    t = x2.shape[0]
    tm = TOKEN_TILE
    nt = o_list[0].shape[1] * o_list[0].shape[2] // tm
    row = lambda i: (i, 0)
    const = lambda i: (0, 0)
    wide = lambda w: pl.BlockSpec((tm, w), row)

    def by_residue(a):
        _, dil, _, width = a.shape
        return pl.BlockSpec((None, dil, tm // dil, width), lambda i: (i // nt, 0, i % nt, 0))

    n_strided = sum(o.shape[1] > 1 for o in o_list)
    return pl.pallas_call(
        _outproj_kernel,
        grid=(t // tm,),
        in_specs=[
            wide(D_MODEL),
            *[by_residue(o) for o in o_list],
            *[by_residue(l) for l in lse_list],
            wide(WIDTH_A), wide(WIDTH_B),
            pl.BlockSpec((WIDTH_A + WIDTH_B, D_MODEL), const),
            pl.BlockSpec((LANES, WIDTH_A), const),
            pl.BlockSpec((1, D_MODEL), const),
            pl.BlockSpec((1, D_MODEL), const),
        ],
        out_specs=wide(D_MODEL),
        out_shape=jax.ShapeDtypeStruct((t, D_MODEL), F32),
        scratch_shapes=[pltpu.VMEM((n_strided * WIDTH_A // LANES, tm, LANES), F32),
                        pltpu.VMEM((n_strided, tm, LANES), F32),
                        pltpu.VMEM((tm, WIDTH_A), BF16)],
        compiler_params=pltpu.CompilerParams(
            dimension_semantics=("arbitrary",), vmem_limit_bytes=VMEM_LIMIT),
        name="outproj",
    )(x2, *o_list, *lse_list, ga, mix_b, w_out_bf16, expand, ln_g, ln_b)


def _constants():
    head_of_lane = np.arange(LANES) // HEAD_DIM
    bd = (head_of_lane[:, None] == head_of_lane[None, :]).astype(np.float32)
    ex = (np.arange(LANES)[:, None] == (np.arange(WIDTH_A) // HEAD_DIM)[None, :]).astype(np.float32)
    return jnp.asarray(bd, BF16), jnp.asarray(ex, BF16)


def _layer(x, w_in_bf16, w_out_bf16, tables, qn, kn, ln_g, ln_b, bd, expand):
    b, n, _ = x.shape
    (qa, ka, va, qa4, ka4, va4, qa16, ka16, va16,
     ga, qbt, gb, kbe, kbo, vbt) = _inproj(x, w_in_bf16, tables, qn, kn, bd)
    mix_b = _gqa(qbt, kbe, kbo, vbt, gb.reshape(b, n, WIDTH_B))
    assert tuple(d for _, d in DILATED_PATTERNS) == (1, 4, 16)
    one = lambda a: a.reshape(b, 1, n, WIDTH_A)
    o_list, lse_list = [], []
    for qkv in ((one(qa), one(ka), one(va)), (qa4, ka4, va4), (qa16, ka16, va16)):
        o, lse = _dilated(*qkv)
        o_list.append(o)
        lse_list.append(lse)
    y = _outproj(x.reshape(b * n, D_MODEL), o_list, lse_list, ga, mix_b.reshape(b * n, WIDTH_B),
                 w_out_bf16, expand, ln_g, ln_b)
    return y.reshape(b, n, D_MODEL)


def kernel(x_prompt, x_sample, w_in, w_out, q_norm, k_norm, ln_g, ln_b):
    bd, expand = _constants()
    tables = _rope_tables(max(x_prompt.shape[1], x_sample.shape[1]))
    rep = LANES // HEAD_DIM
    y_prompt, y_sample = x_prompt, x_sample
    for i in range(w_in.shape[0]):
        w_in_bf16 = w_in[i].astype(BF16)
        w_out_bf16 = w_out[i].astype(BF16)
        qn = jnp.tile(q_norm[i].reshape(1, HEAD_DIM), (1, rep))
        kn = jnp.tile(k_norm[i].reshape(1, HEAD_DIM), (1, rep))
        args = (w_in_bf16, w_out_bf16, tables, qn, kn,
                ln_g[i].reshape(1, D_MODEL), ln_b[i].reshape(1, D_MODEL), bd, expand)
        y_prompt = _layer(y_prompt, *args)
        y_sample = _layer(y_sample, *args)
    return (y_prompt, y_sample)
```
